```python
import jax, jax.numpy as jnp
from jax import lax
import numpy as np

D_MODEL = 2048
BATCH = 1
SEQ = 16384
DEPTH = 2

CHUNK = 64
EPS = 1e-6

D_CONV = D_MODEL // 2
CONV_GROUPS = 8
CONV_WIDTH = 3
D_SGU = D_MODEL // 2
SGU_HEADS = 8
SGU_BLOCK = 128
D_EVEN_IN = 3 * D_CONV + 2 * D_SGU

D_POOL = D_MODEL // 2
POOL_WINDOWS = (2, 4, 8, 16)
POOL_GROUPS = len(POOL_WINDOWS)
POOL_GROUP_DIM = D_POOL // POOL_GROUPS
D_CONF = D_MODEL // 2
CONF_GROUPS = 8
CONF_WIDTH = 31
D_ODD_IN = D_POOL + 2 * D_CONF

N_GROUPS = 4
EXPERTS_PER_GROUP = 8
N_EXPERTS = N_GROUPS * EXPERTS_PER_GROUP
TOP_K = 2
D_EXPERT = 512
MOE_BLOCK = 128

N_EVEN = (DEPTH + 1) // 2
N_ODD = DEPTH // 2

kernel_name = "hybrid_conv_sgu_pool_conformer_hmoe"


def rms_norm(x, g):
    xf = x.astype(jnp.float32)
    y = xf * lax.rsqrt(jnp.mean(xf * xf, axis=-1, keepdims=True) + EPS)
    return (y * g.astype(jnp.float32)).astype(x.dtype)


def layer_norm(x, g, b):
    xf = x.astype(jnp.float32)
    mu = jnp.mean(xf, axis=-1, keepdims=True)
    var = jnp.mean(jnp.square(xf - mu), axis=-1, keepdims=True)
    y = (xf - mu) * lax.rsqrt(var + EPS) * g.astype(jnp.float32) + b.astype(jnp.float32)
    return y.astype(x.dtype)


def causal_depthwise_conv(x, w):
    k, c = w.shape
    return lax.conv_general_dilated(
        x, w[:, None, :].astype(x.dtype), window_strides=(1,),
        padding=[(k - 1, 0)], dimension_numbers=("NWC", "WIO", "NWC"),
        feature_group_count=c)


def chunk_causal_mask(n):
    pos = jnp.arange(n)
    return (pos[None, :] // CHUNK) <= (pos[:, None] // CHUNK)


def pool_mixer(xc, pool_w, pool_scale):
    b, s, c = xc.shape
    xg = xc.astype(jnp.float32).reshape(b, s, POOL_GROUPS, POOL_GROUP_DIM)
    cs = jnp.cumsum(xg, axis=1)
    steps = jnp.arange(1, s + 1)
    outs = []
    for g, w in enumerate(POOL_WINDOWS):
        csg = cs[:, :, g]
        prev = jnp.pad(csg, ((0, 0), (w, 0), (0, 0)))[:, :s]
        count = jnp.minimum(steps, w).astype(jnp.float32)[None, :, None]
        outs.append((csg - prev) / count - xg[:, :, g])
    d = jnp.stack(outs, axis=2)
    y = jnp.einsum("bsgc,gcd->bsgd", d, pool_w.astype(jnp.float32)).reshape(b, s, c)
    return (y * pool_scale.astype(jnp.float32)).astype(xc.dtype)


def even_mixer(h, w_in, conv_w, sgu_ln_g, sgu_ln_b, sgu_w, sgu_b, w_out):
    z = h @ w_in
    b_gate = z[..., :D_CONV]
    c_gate = z[..., D_CONV:2 * D_CONV]
    hc = z[..., 2 * D_CONV:3 * D_CONV]
    zs = jax.nn.gelu(z[..., 3 * D_CONV:])
    y_a = b_gate * causal_depthwise_conv(c_gate * hc, conv_w)
    u, v = zs[..., :D_SGU], zs[..., D_SGU:]
    v = layer_norm(v, sgu_ln_g, sgu_ln_b)
    bsz, s, _ = v.shape
    nb = s // SGU_BLOCK
    v = v.reshape(bsz, nb, SGU_BLOCK, SGU_HEADS, D_SGU // SGU_HEADS)
    w_s = jnp.where(chunk_causal_mask(SGU_BLOCK)[None], sgu_w, 0).astype(v.dtype)
    sv = jnp.einsum("htp,bnphc->bnthc", w_s, v)
    sv = sv + sgu_b.T.astype(v.dtype)[None, None, :, :, None]
    y_b = u * sv.reshape(bsz, s, D_SGU)
    return jnp.concatenate([y_a, y_b], axis=-1) @ w_out


def odd_mixer(h, w_in, pool_w, pool_scale, dw_w, dw_b, ln_g, ln_b, w_out):
    z = h @ w_in
    xc = z[..., :D_POOL]
    za = z[..., D_POOL:D_POOL + D_CONF]
    zg = z[..., D_POOL + D_CONF:]
    y_c = pool_mixer(xc, pool_w, pool_scale)
    glu = za * jax.nn.sigmoid(zg)
    d = causal_depthwise_conv(glu, dw_w) + dw_b.astype(glu.dtype)
    y_d = jax.nn.silu(layer_norm(d, ln_g, ln_b))
    return jnp.concatenate([y_c, y_d], axis=-1) @ w_out


def hier_moe(x, w_group, b_group, w_router, b_router, w1, w3, w2):
    bsz, s, d = x.shape
    t = bsz * s
    xf = x.reshape(t, d)
    g_logits = (xf @ w_group).astype(jnp.float32) + b_group.astype(jnp.float32)
    g_prob = jax.nn.softmax(g_logits, axis=-1)
    _, g_idx = lax.top_k(g_logits, 1)
    p_group = jnp.take_along_axis(g_prob, g_idx, axis=1)[:, 0]
    e_logits = (xf @ w_router).astype(jnp.float32) + b_router.astype(jnp.float32)
    e_logits = e_logits.reshape(t, N_GROUPS, EXPERTS_PER_GROUP)
    e_logits = jnp.take_along_axis(e_logits, g_idx[:, :, None], axis=1)[:, 0]
    e_prob = jax.nn.softmax(e_logits, axis=-1)
    top_p, top_e = lax.top_k(e_prob, TOP_K)
    top_p = top_p / jnp.sum(top_p, axis=-1, keepdims=True)
    gate = p_group[:, None] * top_p
    expert = g_idx * EXPERTS_PER_GROUP + top_e

    n_slots = t * TOP_K
    flat_e = expert.reshape(-1)
    flat_tok = jnp.arange(n_slots, dtype=jnp.int32) // TOP_K
    flat_gate = gate.reshape(-1)
    order = jnp.argsort(flat_e)
    sorted_e = flat_e[order]
    counts = jnp.bincount(flat_e, length=N_EXPERTS)
    padded = ((counts + MOE_BLOCK - 1) // MOE_BLOCK) * MOE_BLOCK
    pad_end = jnp.cumsum(padded)
    pad_start = pad_end - padded
    start = jnp.cumsum(counts) - counts
    rank = jnp.arange(n_slots) - start[sorted_e]
    dest = pad_start[sorted_e] + rank
    cap = n_slots + N_EXPERTS * MOE_BLOCK
    n_blk = cap // MOE_BLOCK
    slot_tok = jnp.full((cap,), t, dtype=jnp.int32).at[dest].set(flat_tok[order])
    slot_gate = jnp.zeros((cap,), jnp.float32).at[dest].set(flat_gate[order])
    blk_e = jnp.minimum(
        jnp.searchsorted(pad_end, jnp.arange(n_blk) * MOE_BLOCK, side="right"),
        N_EXPERTS - 1)
    x_pad = jnp.concatenate([xf, jnp.zeros((1, d), xf.dtype)], axis=0)
    xb = x_pad[slot_tok].reshape(n_blk, MOE_BLOCK, d)

    def expert_block(args):
        xblk, e = args
        hid = jax.nn.silu(xblk @ w1[e]) * (xblk @ w3[e])
        return hid @ w2[e]

    yb = lax.map(expert_block, (xb, blk_e)).reshape(cap, d)
    yb = yb * slot_gate[:, None].astype(yb.dtype)
    out = jnp.zeros((t + 1, d), yb.dtype).at[slot_tok].add(yb)[:t]
    return out.reshape(bsz, s, d)


def setup_inputs(seed: int = 0) -> dict:
    key = jax.random.key(seed)
    ks = jax.random.split(key, 32)
    f32 = jnp.float32

    def nrm(k, shape, scale):
        return jax.random.normal(k, shape, f32) * scale

    def gain(k, shape):
        return 1.0 + 0.02 * jax.random.normal(k, shape, f32)

    return {
        "x": jax.random.normal(ks[0], (BATCH, SEQ, D_MODEL), f32),
        "ln_mix": gain(ks[1], (DEPTH, D_MODEL)),
        "ln_ffn": gain(ks[2], (DEPTH, D_MODEL)),
        "ln_final": gain(ks[3], (D_MODEL,)),
        "ev_w_in": nrm(ks[4], (N_EVEN, D_MODEL, D_EVEN_IN), D_MODEL ** -0.5),
        "ev_conv_w": nrm(ks[5], (N_EVEN, CONV_WIDTH, D_CONV), CONV_WIDTH ** -0.5),
        "ev_sgu_ln_g": gain(ks[6], (N_EVEN, D_SGU)),
        "ev_sgu_ln_b": nrm(ks[7], (N_EVEN, D_SGU), 0.02),
        "ev_sgu_w": nrm(ks[8], (N_EVEN, SGU_HEADS, SGU_BLOCK, SGU_BLOCK), SGU_BLOCK ** -0.5),
        "ev_sgu_b": gain(ks[9], (N_EVEN, SGU_HEADS, SGU_BLOCK)),
        "ev_w_out": nrm(ks[10], (N_EVEN, D_CONV + D_SGU, D_MODEL), (D_CONV + D_SGU) ** -0.5),
        "od_w_in": nrm(ks[11], (N_ODD, D_MODEL, D_ODD_IN), D_MODEL ** -0.5),
        "od_pool_w": nrm(ks[12], (N_ODD, POOL_GROUPS, POOL_GROUP_DIM, POOL_GROUP_DIM), POOL_GROUP_DIM ** -0.5),
        "od_pool_scale": gain(ks[13], (N_ODD, D_POOL)),
        "od_dw_w": nrm(ks[14], (N_ODD, CONF_WIDTH, D_CONF), CONF_WIDTH ** -0.5),
        "od_dw_b": nrm(ks[15], (N_ODD, D_CONF), 0.02),
        "od_ln_g": gain(ks[16], (N_ODD, D_CONF)),
        "od_ln_b": nrm(ks[17], (N_ODD, D_CONF), 0.02),
        "od_w_out": nrm(ks[18], (N_ODD, D_POOL + D_CONF, D_MODEL), (D_POOL + D_CONF) ** -0.5),
        "moe_w_group": nrm(ks[19], (DEPTH, D_MODEL, N_GROUPS), D_MODEL ** -0.5),
        "moe_b_group": nrm(ks[20], (DEPTH, N_GROUPS), 0.01),
        "moe_w_router": nrm(ks[21], (DEPTH, D_MODEL, N_EXPERTS), D_MODEL ** -0.5),
        "moe_b_router": nrm(ks[22], (DEPTH, N_EXPERTS), 0.01),
        "moe_w1": nrm(ks[23], (DEPTH, N_EXPERTS, D_MODEL, D_EXPERT), D_MODEL ** -0.5),
        "moe_w3": nrm(ks[24], (DEPTH, N_EXPERTS, D_MODEL, D_EXPERT), D_MODEL ** -0.5),
        "moe_w2": nrm(ks[25], (DEPTH, N_EXPERTS, D_EXPERT, D_MODEL), D_EXPERT ** -0.5),
    }


def reference(x, ln_mix, ln_ffn, ln_final,
              ev_w_in, ev_conv_w, ev_sgu_ln_g, ev_sgu_ln_b, ev_sgu_w, ev_sgu_b, ev_w_out,
              od_w_in, od_pool_w, od_pool_scale, od_dw_w, od_dw_b, od_ln_g, od_ln_b, od_w_out,
              moe_w_group, moe_b_group, moe_w_router, moe_b_router, moe_w1, moe_w3, moe_w2):
    for layer in range(DEPTH):
        i = layer // 2
        h = rms_norm(x, ln_mix[layer])
        if layer % 2 == 0:
            x = x + even_mixer(h, ev_w_in[i], ev_conv_w[i], ev_sgu_ln_g[i], ev_sgu_ln_b[i],
                               ev_sgu_w[i], ev_sgu_b[i], ev_w_out[i])
        else:
            x = x + odd_mixer(h, od_w_in[i], od_pool_w[i], od_pool_scale[i], od_dw_w[i],
                              od_dw_b[i], od_ln_g[i], od_ln_b[i], od_w_out[i])
        h = rms_norm(x, ln_ffn[layer])
        x = x + hier_moe(h, moe_w_group[layer], moe_b_group[layer], moe_w_router[layer],
                         moe_b_router[layer], moe_w1[layer], moe_w3[layer], moe_w2[layer])
    return rms_norm(x, ln_final)
```

```python
import functools
import math

import jax
import jax.numpy as jnp
from jax import lax
from jax.experimental import pallas as pl
from jax.experimental.pallas import tpu as pltpu

F32 = jnp.float32
BF16 = jnp.bfloat16
I32 = jnp.int32

EPS = 1e-6
CHUNK = 64
SGU_BLOCK = 128
SGU_HEADS = 8
POOL_WINDOWS = (2, 4, 8, 16)
N_GROUPS = 4
EXPERTS_PER_GROUP = 8
N_EXPERTS = N_GROUPS * EXPERTS_PER_GROUP
TOP_K = 2

LANES = 128
SUBLANES = 8
VMEM_CAP_BYTES = 56 * 1024 * 1024

ROUTER_ROWS = SUBLANES + N_EXPERTS
NEG_BIG = -1e30


def _vmem_limit(nbytes):
    return int(min(VMEM_CAP_BYTES, nbytes * 3 // 2 + (8 << 20)))


def _params(n_axes, nbytes):
    return pltpu.CompilerParams(
        dimension_semantics=("arbitrary",) * n_axes,
        vmem_limit_bytes=_vmem_limit(nbytes))


def _sigmoid(x):
    return 1.0 / (1.0 + jnp.exp(-x))


def _gelu_tanh(x):
    c = math.sqrt(2.0 / math.pi)
    return x * (0.5 * (1.0 + jnp.tanh(c * (x + 0.044715 * (x * x * x)))))


def _layer_norm(v, g, b):
    mu = jnp.mean(v, axis=-1, keepdims=True)
    vc = v - mu
    var = jnp.mean(vc * vc, axis=-1, keepdims=True)
    return vc * lax.rsqrt(var + EPS) * g + b


def _rms_norm(x, g):
    ms = jnp.mean(x * x, axis=-1, keepdims=True)
    return x * lax.rsqrt(ms + EPS) * g


def _norm_matmul_body(x_ref, g_ref, w_ref, o_ref, hn_ref):
    @pl.when(pl.program_id(1) == 0)
    def _():
        hn_ref[...] = _rms_norm(x_ref[...], g_ref[...]).astype(BF16)

    o_ref[...] = jnp.dot(hn_ref[...], w_ref[...], preferred_element_type=F32)


def _norm_matmul(x, g, w, *, tm, tn):
    t, d = x.shape
    n = w.shape[1]
    nbytes = 2 * (tm * d * 4 + d * tn * 2 + tm * tn * 4) + tm * d * 2
    return pl.pallas_call(
        _norm_matmul_body,
        grid=(t // tm, n // tn),
        in_specs=[pl.BlockSpec((tm, d), lambda i, j: (i, 0)),
                  pl.BlockSpec((1, d), lambda i, j: (0, 0)),
                  pl.BlockSpec((d, tn), lambda i, j: (0, j))],
        out_specs=pl.BlockSpec((tm, tn), lambda i, j: (i, j)),
        out_shape=jax.ShapeDtypeStruct((t, n), F32),
        scratch_shapes=[pltpu.VMEM((tm, d), BF16)],
        compiler_params=_params(2, nbytes),
        name="norm_matmul",
    )(x, g.reshape(1, d), w)


EVEN_HALO = SUBLANES


def _even_mixer_body(b_ref, c_ref, h_ref, u_ref, v_ref, cw_ref, lg_ref, lb_ref,
                     sw_ref, sb_ref, o_ref, pbuf):
    tm, c = b_ref.shape
    kw = cw_ref.shape[0]
    i = pl.program_id(0)

    @pl.when(i == 0)
    def _():
        pbuf[0:EVEN_HALO, :] = jnp.zeros((EVEN_HALO, c), F32)

    p = c_ref[...] * h_ref[...]
    pbuf[EVEN_HALO:EVEN_HALO + tm, :] = p
    conv = cw_ref[kw - 1:kw, :] * p
    for k in range(kw - 1):
        conv = conv + cw_ref[k:k + 1, :] * pbuf[pl.ds(EVEN_HALO - (kw - 1) + k, tm), :]
    o_ref[:, 0:c] = (b_ref[...] * conv).astype(BF16)
    pbuf[0:EVEN_HALO, :] = pbuf[tm:tm + EVEN_HALO, :]

    u = _gelu_tanh(u_ref[...])
    vn = _layer_norm(_gelu_tanh(v_ref[...]), lg_ref[...], lb_ref[...]).astype(BF16)
    tpos = lax.broadcasted_iota(I32, (SGU_BLOCK, SGU_BLOCK), 0)
    spos = lax.broadcasted_iota(I32, (SGU_BLOCK, SGU_BLOCK), 1)
    mask = (spos // CHUNK) <= (tpos // CHUNK)
    hd = c // SGU_HEADS
    for h in range(SGU_HEADS):
        wh = jnp.where(mask, sw_ref[h], 0.0).astype(BF16)
        bh = sb_ref[:, h:h + 1]
        cols = slice(h * hd, (h + 1) * hd)
        for n in range(tm // SGU_BLOCK):
            rows = slice(n * SGU_BLOCK, (n + 1) * SGU_BLOCK)
            sv = jnp.dot(wh, vn[rows, cols], preferred_element_type=F32) + bh
            o_ref[rows, c + h * hd:c + (h + 1) * hd] = (u[rows, cols] * sv).astype(BF16)


def _even_mixer(z, conv_w, ln_g, ln_b, sgu_w, sgu_b, *, tm):
    t = z.shape[0]
    c = conv_w.shape[1]
    zspec = lambda col: pl.BlockSpec((tm, c), lambda i, col=col: (i, col))
    const = lambda shape: pl.BlockSpec(shape, lambda i: (0,) * len(shape))
    nbytes = 2 * (5 * tm * c * 4 + tm * 2 * c * 2) + (tm + EVEN_HALO) * c * 4 + 8 * tm * c * 4
    return pl.pallas_call(
        _even_mixer_body,
        grid=(t // tm,),
        in_specs=[zspec(0), zspec(1), zspec(2), zspec(3), zspec(4),
                  const(conv_w.shape), const((1, c)), const((1, c)),
                  const(sgu_w.shape), const((SGU_BLOCK, SGU_HEADS))],
        out_specs=pl.BlockSpec((tm, 2 * c), lambda i: (i, 0)),
        out_shape=jax.ShapeDtypeStruct((t, 2 * c), BF16),
        scratch_shapes=[pltpu.VMEM((tm + EVEN_HALO, c), F32)],
        compiler_params=_params(1, nbytes),
        name="even_mixer",
    )(z, z, z, z, z, conv_w, ln_g.reshape(1, c), ln_b.reshape(1, c), sgu_w, sgu_b.T)


POOL_HALO = 16
CONF_HALO = 32
CONV_ROWS = 128


def _odd_mixer_body(x_ref, za_ref, zg_ref, pw_ref, ps_ref, dw_ref, db_ref, lg_ref, lb_ref,
                    o_ref, xbuf, gbuf, dbuf):
    tm, c = x_ref.shape
    i = pl.program_id(0)

    @pl.when(i == 0)
    def _():
        xbuf[0:POOL_HALO, :] = jnp.zeros((POOL_HALO, c), F32)
        gbuf[0:CONF_HALO, :] = jnp.zeros((CONF_HALO, c), F32)

    x = x_ref[...]
    xbuf[POOL_HALO:POOL_HALO + tm, :] = x
    step = i * tm + lax.broadcasted_iota(I32, (tm, 1), 0) + 1
    gd = c // len(POOL_WINDOWS)
    for g, w in enumerate(POOL_WINDOWS):
        cols = slice(g * gd, (g + 1) * gd)
        xg = x[:, cols]
        s = xg
        for j in range(1, w):
            s = s + xbuf[pl.ds(POOL_HALO - j, tm), cols]
        inv_count = 1.0 / jnp.minimum(step, w).astype(F32)
        d = s * inv_count - xg
        y = jnp.dot(d.astype(BF16), pw_ref[g].astype(BF16), preferred_element_type=F32)
        o_ref[:, cols] = (y * ps_ref[:, cols]).astype(BF16)
    xbuf[0:POOL_HALO, :] = xbuf[tm:tm + POOL_HALO, :]

    kw = dw_ref.shape[0]
    gbuf[CONF_HALO:CONF_HALO + tm, :] = za_ref[...] * _sigmoid(zg_ref[...])

    def conv_cols(cb, carry):
        c0 = pl.multiple_of(cb * LANES, LANES)
        for r0 in range(0, tm, CONV_ROWS):
            acc = jnp.broadcast_to(db_ref[:, pl.ds(c0, LANES)], (CONV_ROWS, LANES))
            for k in range(kw):
                tap = gbuf[pl.ds(r0 + CONF_HALO - (kw - 1) + k, CONV_ROWS), pl.ds(c0, LANES)]
                acc = acc + dw_ref[k:k + 1, pl.ds(c0, LANES)] * tap
            dbuf[r0:r0 + CONV_ROWS, pl.ds(c0, LANES)] = acc
        return carry

    lax.fori_loop(0, c // LANES, conv_cols, 0)
    gbuf[0:CONF_HALO, :] = gbuf[tm:tm + CONF_HALO, :]
    yn = _layer_norm(dbuf[...], lg_ref[...], lb_ref[...])
    o_ref[:, c:2 * c] = (yn * _sigmoid(yn)).astype(BF16)


def _odd_mixer(z, pool_w, pool_scale, dw_w, dw_b, ln_g, ln_b, *, tm):
    t = z.shape[0]
    c = dw_w.shape[1]
    zspec = lambda col: pl.BlockSpec((tm, c), lambda i, col=col: (i, col))
    const = lambda shape: pl.BlockSpec(shape, lambda i: (0,) * len(shape))
    nbytes = (2 * (3 * tm * c * 4 + tm * 2 * c * 2) + (3 * tm + POOL_HALO + CONF_HALO) * c * 4
              + 6 * tm * c * 4)
    return pl.pallas_call(
        _odd_mixer_body,
        grid=(t // tm,),
        in_specs=[zspec(0), zspec(1), zspec(2),
                  const(pool_w.shape), const((1, c)), const(dw_w.shape), const((1, c)),
                  const((1, c)), const((1, c))],
        out_specs=pl.BlockSpec((tm, 2 * c), lambda i: (i, 0)),
        out_shape=jax.ShapeDtypeStruct((t, 2 * c), BF16),
        scratch_shapes=[pltpu.VMEM((tm + POOL_HALO, c), F32),
                        pltpu.VMEM((tm + CONF_HALO, c), F32),
                        pltpu.VMEM((tm, c), F32)],
        compiler_params=_params(1, nbytes),
        name="odd_mixer",
    )(z, z, z, pool_w, pool_scale.reshape(1, c), dw_w, dw_b.reshape(1, c),
      ln_g.reshape(1, c), ln_b.reshape(1, c))


def _proj_route_body(y_ref, w_ref, x_ref, g_ref, wr_ref, br_ref,
                     x1_ref, hn_ref, meta_ref, gate_ref, cnt_ref, run_ref):
    tm = x_ref.shape[0]
    i = pl.program_id(0)

    @pl.when(i == 0)
    def _():
        run_ref[...] = jnp.zeros(run_ref.shape, F32)

    x1 = x_ref[...] + jnp.dot(y_ref[...], w_ref[...], preferred_element_type=F32)
    x1_ref[...] = x1
    hn = _rms_norm(x1, g_ref[...])
    hn_ref[...] = hn

    logits = lax.dot_general(wr_ref[...], hn, (((1,), (1,)), ((), ())),
                             precision=lax.Precision.HIGHEST,
                             preferred_element_type=F32) + br_ref[...]
    row8 = lax.broadcasted_iota(I32, (SUBLANES, tm), 0)
    row = lax.broadcasted_iota(I32, (N_EXPERTS, tm), 0)

    gl = jnp.where(row8 < N_GROUPS, logits[0:SUBLANES], NEG_BIG)
    gmax = jnp.max(gl, axis=0, keepdims=True)
    gidx = jnp.min(jnp.where(gl == gmax, row8, N_EXPERTS), axis=0, keepdims=True)
    p_group = 1.0 / jnp.sum(jnp.exp(gl - gmax), axis=0, keepdims=True)

    el = jnp.where(row // EXPERTS_PER_GROUP == gidx, logits[SUBLANES:ROUTER_ROWS], NEG_BIG)
    m0 = jnp.max(el, axis=0, keepdims=True)
    e0 = jnp.min(jnp.where(el == m0, row, N_EXPERTS), axis=0, keepdims=True)
    el1 = jnp.where(row == e0, NEG_BIG, el)
    m1 = jnp.max(el1, axis=0, keepdims=True)
    e1 = jnp.min(jnp.where(el1 == m1, row, N_EXPERTS), axis=0, keepdims=True)
    ratio = jnp.exp(m1 - m0)
    gate0 = p_group / (1.0 + ratio)
    gate1 = p_group * ratio / (1.0 + ratio)

    sel0 = row == e0
    sel1 = row == e1
    src = lax.broadcasted_iota(I32, (tm, tm), 0)
    dst = lax.broadcasted_iota(I32, (tm, tm), 1)
    before = (src < dst).astype(BF16)
    pre0 = jnp.dot(sel0.astype(BF16), before, preferred_element_type=F32)
    pre1 = jnp.dot(sel1.astype(BF16), before, preferred_element_type=F32)
    cnt0 = jnp.sum(sel0.astype(F32), axis=1, keepdims=True)
    cnt1 = jnp.sum(sel1.astype(F32), axis=1, keepdims=True)
    run = run_ref[:, 0:1]
    rank0 = jnp.sum(jnp.where(sel0, pre0 + run, 0.0), axis=0, keepdims=True)
    rank1 = jnp.sum(jnp.where(sel1, pre1 + (run + cnt0), 0.0), axis=0, keepdims=True)
    run_ref[...] = run_ref[...] + (cnt0 + cnt1)
    cnt_ref[...] = run_ref[...]

    meta = jnp.where(row8 == 0, e0, jnp.where(row8 == 1, e1, jnp.where(
        row8 == 2, rank0.astype(I32), jnp.where(row8 == 3, rank1.astype(I32), 0))))
    meta_ref[...] = meta
    gate_ref[...] = jnp.where(row8 == 0, gate0, jnp.where(row8 == 1, gate1, 0.0))


def _proj_route(ycat, w_out, x, g, wr_t, br, *, tm):
    t, d = x.shape
    k = ycat.shape[1]
    const = lambda shape: pl.BlockSpec(shape, lambda i: (0,) * len(shape))
    nbytes = (2 * (tm * k * 2 + k * d * 2 + 3 * tm * d * 4 + ROUTER_ROWS * d * 4)
              + 4 * tm * d * 4 + tm * tm * 4)
    return pl.pallas_call(
        _proj_route_body,
        grid=(t // tm,),
        in_specs=[pl.BlockSpec((tm, k), lambda i: (i, 0)), const((k, d)),
                  pl.BlockSpec((tm, d), lambda i: (i, 0)), const((1, d)),
                  const((ROUTER_ROWS, d)), const((ROUTER_ROWS, 1))],
        out_specs=[pl.BlockSpec((tm, d), lambda i: (i, 0)),
                   pl.BlockSpec((tm, d), lambda i: (i, 0)),
                   pl.BlockSpec((SUBLANES, tm), lambda i: (0, i)),
                   pl.BlockSpec((SUBLANES, tm), lambda i: (0, i)),
                   const((N_EXPERTS, LANES))],
        out_shape=[jax.ShapeDtypeStruct((t, d), F32),
                   jax.ShapeDtypeStruct((t, d), F32),
                   jax.ShapeDtypeStruct((SUBLANES, t), I32),
                   jax.ShapeDtypeStruct((SUBLANES, t), F32),
                   jax.ShapeDtypeStruct((N_EXPERTS, LANES), F32)],
        scratch_shapes=[pltpu.VMEM((N_EXPERTS, LANES), F32)],
        compiler_params=_params(1, nbytes),
        name="proj_route",
    )(ycat, w_out, x, g.reshape(1, d), wr_t, br)


def _row_copy_wait(src_ref, dst_ref, sem, n):
    def body(r, carry):
        pltpu.make_async_copy(src_ref.at[pl.ds(0, 1), :], dst_ref.at[pl.ds(0, 1), :], sem).wait()
        return carry
    lax.fori_loop(0, n, body, 0)


def _scatter_body(pos_ref, h_ref, xs_ref, sem):
    tm = h_ref.shape[0]

    def issue(r, carry):
        for k in range(TOP_K):
            p = pos_ref[0, k, r]
            pltpu.make_async_copy(h_ref.at[pl.ds(r, 1), :], xs_ref.at[pl.ds(p, 1), :], sem).start()
        return carry

    lax.fori_loop(0, tm, issue, 0)
    _row_copy_wait(h_ref, xs_ref, sem, TOP_K * tm)


def _scatter_rows(hn, pos3, *, tm):
    t, d = hn.shape
    return pl.pallas_call(
        _scatter_body,
        grid=(t // tm,),
        in_specs=[pl.BlockSpec((1, TOP_K, tm), lambda i: (i, 0, 0), memory_space=pltpu.SMEM),
                  pl.BlockSpec((tm, d), lambda i: (i, 0))],
        out_specs=pl.BlockSpec(memory_space=pl.ANY),
        out_shape=jax.ShapeDtypeStruct((TOP_K * t, d), hn.dtype),
        scratch_shapes=[pltpu.SemaphoreType.DMA],
        compiler_params=_params(1, 2 * tm * d * 4),
        name="scatter_rows",
    )(pos3, hn)


def _expert_body(vblk, vexp, vfirst, vnew, vvalid, estart,
                 x_ref, w1_ref, w3_ref, w2_ref, o_ref, w1b, w3b, w2b):
    bm = x_ref.shape[0]
    v = pl.program_id(0)

    @pl.when(vvalid[v] == 1)
    def _():
        @pl.when(vnew[v] == 1)
        def _():
            w1b[...] = w1_ref[0, 0].astype(BF16)
            w3b[...] = w3_ref[0, 0].astype(BF16)
            w2b[...] = w2_ref[0, 0].astype(BF16)

        xb = x_ref[...].astype(BF16)
        h1 = jnp.dot(xb, w1b[...], preferred_element_type=F32)
        h3 = jnp.dot(xb, w3b[...], preferred_element_type=F32)
        hid = (h1 * _sigmoid(h1)) * h3
        y = jnp.dot(hid.astype(BF16), w2b[...], preferred_element_type=F32)

        @pl.when(vfirst[v] == 1)
        def _():
            o_ref[...] = y

        @pl.when(vfirst[v] == 0)
        def _():
            e = vexp[v]
            r = vblk[v] * bm + lax.broadcasted_iota(I32, (bm, 1), 0)
            mine = (r >= estart[e]) & (r < estart[e + 1])
            o_ref[...] = jnp.where(mine, y, o_ref[...])


def _experts(xs, w1, w3, w2, tables, *, layer, bm):
    n, d = xs.shape
    f = w1.shape[3]
    n_visits = tables[0].shape[0]
    nbytes = 2 * (bm * d * 4 * 2 + 3 * d * f * 4) + 3 * d * f * 2 + 6 * bm * d * 4
    wmap = lambda v, blk, exp, *_: (layer, exp[v], 0, 0)
    grid_spec = pltpu.PrefetchScalarGridSpec(
        num_scalar_prefetch=len(tables),
        grid=(n_visits,),
        in_specs=[pl.BlockSpec((bm, d), lambda v, blk, exp, *_: (blk[v], 0)),
                  pl.BlockSpec((1, 1, d, f), wmap),
                  pl.BlockSpec((1, 1, d, f), wmap),
                  pl.BlockSpec((1, 1, f, d), wmap)],
        out_specs=pl.BlockSpec((bm, d), lambda v, blk, exp, *_: (blk[v], 0)),
        scratch_shapes=[pltpu.VMEM((d, f), BF16), pltpu.VMEM((d, f), BF16),
                        pltpu.VMEM((f, d), BF16)])
    return pl.pallas_call(
        _expert_body,
        grid_spec=grid_spec,
        out_shape=jax.ShapeDtypeStruct((n, d), F32),
        compiler_params=_params(1, nbytes),
        name="experts",
    )(*tables, xs, w1, w3, w2)


def _visit_tables(counts, n_slots, bm):
    n_blocks = n_slots // bm
    n_visits = n_blocks + N_EXPERTS - 1
    end = jnp.cumsum(counts)
    start = end - counts
    first_blk = start // bm
    last_blk = jnp.maximum(end - 1, 0) // bm
    per_e = jnp.where(counts > 0, last_blk - first_blk + 1, 0)
    cum = jnp.cumsum(per_e)
    total = cum[-1]
    v = jnp.minimum(jnp.arange(n_visits, dtype=I32), total - 1)
    e = jnp.searchsorted(cum, v, side="right").astype(I32)
    blk = (first_blk[e] + (v - (cum[e] - per_e[e]))).astype(I32)
    valid = (jnp.arange(n_visits) < total).astype(I32)
    prev_blk = jnp.concatenate([jnp.full((1,), -1, I32), blk[:-1]])
    prev_e = jnp.concatenate([jnp.full((1,), -1, I32), e[:-1]])
    first = (blk != prev_blk).astype(I32)
    new = (e != prev_e).astype(I32)
    estart = jnp.concatenate([start, end[-1:]]).astype(I32)
    return blk, e, first, new, valid, estart


def _combine_body(pos_ref, x_ref, gt_ref, g_ref, ys_ref, o_ref, ybuf, sem, *, final_norm):
    tm = x_ref.shape[0]

    def issue(r, carry):
        for k in range(TOP_K):
            p = pos_ref[0, k, r]
            pltpu.make_async_copy(ys_ref.at[pl.ds(p, 1), :], ybuf.at[k, pl.ds(r, 1), :], sem).start()
        return carry

    lax.fori_loop(0, tm, issue, 0)
    _row_copy_wait(ys_ref, ybuf.at[0], sem, TOP_K * tm)
    x2 = x_ref[...] + gt_ref[:, 0:1] * ybuf[0] + gt_ref[:, 1:2] * ybuf[1]
    if final_norm:
        x2 = _rms_norm(x2, g_ref[...])
    o_ref[...] = x2


def _combine(x1, ys, pos3, gates_t, g_final, *, tm, final_norm):
    t, d = x1.shape
    nbytes = 2 * (2 * tm * d * 4 + tm * LANES * 4) + TOP_K * tm * d * 4 + 2 * tm * d * 4
    return pl.pallas_call(
        functools.partial(_combine_body, final_norm=final_norm),
        grid=(t // tm,),
        in_specs=[pl.BlockSpec((1, TOP_K, tm), lambda i: (i, 0, 0), memory_space=pltpu.SMEM),
                  pl.BlockSpec((tm, d), lambda i: (i, 0)),
                  pl.BlockSpec((tm, TOP_K), lambda i: (i, 0)),
                  pl.BlockSpec((1, d), lambda i: (0, 0)),
                  pl.BlockSpec(memory_space=pl.ANY)],
        out_specs=pl.BlockSpec((tm, d), lambda i: (i, 0)),
        out_shape=jax.ShapeDtypeStruct((t, d), F32),
        scratch_shapes=[pltpu.VMEM((TOP_K, tm, d), F32), pltpu.SemaphoreType.DMA],
        compiler_params=_params(1, nbytes),
        name="combine",
    )(pos3, x1, gates_t, g_final.reshape(1, d), ys)


def _tiles(t):
    return dict(tm_in=min(512, t), tn_in=1024, tm_mix=min(256, t), tm_route=min(256, t),
                tm_rows=min(256, t), bm=min(256, TOP_K * t))


def _moe(ycat, w_out, x, ln_g, w_group, b_group, w_router, b_router, w1, w3, w2, layer,
         g_final, final_norm, tl):
    t, d = x.shape
    pad = jnp.zeros((SUBLANES - N_GROUPS, d), F32)
    wr_t = jnp.concatenate([w_group.T, pad, w_router.T], axis=0)
    br = jnp.concatenate([b_group, jnp.zeros((SUBLANES - N_GROUPS,), F32), b_router]).reshape(-1, 1)
    x1, hn, meta, gates, cnt = _proj_route(ycat, w_out, x, ln_g, wr_t, br, tm=tl["tm_route"])

    counts = cnt[:, 0].astype(I32)
    start = jnp.cumsum(counts) - counts
    pos = start[meta[0:TOP_K]] + meta[TOP_K:2 * TOP_K]
    tm = tl["tm_rows"]
    pos3 = pos.reshape(TOP_K, t // tm, tm).transpose(1, 0, 2)

    xs = _scatter_rows(hn, pos3, tm=tm)
    tables = _visit_tables(counts, TOP_K * t, tl["bm"])
    ys = _experts(xs, w1, w3, w2, tables, layer=layer, bm=tl["bm"])
    return _combine(x1, ys, pos3, gates[0:TOP_K].T, g_final, tm=tm, final_norm=final_norm)


def kernel(x, ln_mix, ln_ffn, ln_final, ev_w_in, ev_conv_w, ev_sgu_ln_g, ev_sgu_ln_b, ev_sgu_w, ev_sgu_b, ev_w_out, od_w_in, od_pool_w, od_pool_scale, od_dw_w, od_dw_b, od_ln_g, od_ln_b, od_w_out, moe_w_group, moe_b_group, moe_w_router, moe_b_router, moe_w1, moe_w3, moe_w2):
    bsz, s, d = x.shape
    assert bsz == 1, "token mixers carry state across row tiles of one sequence"
    depth = ln_mix.shape[0]
    tl = _tiles(s)
    xf = x.reshape(s, d)
    for layer in range(depth):
        i = layer // 2
        last = layer == depth - 1
        if layer % 2 == 0:
            z = _norm_matmul(xf, ln_mix[layer], ev_w_in[i].astype(BF16), tm=tl["tm_in"], tn=tl["tn_in"])
            ycat = _even_mixer(z, ev_conv_w[i], ev_sgu_ln_g[i], ev_sgu_ln_b[i], ev_sgu_w[i],
                               ev_sgu_b[i], tm=tl["tm_mix"])
            w_out = ev_w_out[i]
        else:
            z = _norm_matmul(xf, ln_mix[layer], od_w_in[i].astype(BF16), tm=tl["tm_in"], tn=tl["tn_in"])
            ycat = _odd_mixer(z, od_pool_w[i], od_pool_scale[i], od_dw_w[i], od_dw_b[i],
                              od_ln_g[i], od_ln_b[i], tm=tl["tm_mix"])
            w_out = od_w_out[i]
        xf = _moe(ycat, w_out.astype(BF16), xf, ln_ffn[layer], moe_w_group[layer], moe_b_group[layer],
                  moe_w_router[layer], moe_b_router[layer], moe_w1, moe_w3, moe_w2, layer,
                  ln_final, last, tl)
    return xf.reshape(bsz, s, d)
```

```python
import functools
import math

import jax
import jax.numpy as jnp
from jax import lax
from jax.experimental import pallas as pl
from jax.experimental.pallas import tpu as pltpu

F32 = jnp.float32
BF16 = jnp.bfloat16
I32 = jnp.int32

EPS = 1e-6
CHUNK = 64
SGU_BLOCK = 128
SGU_HEADS = 8
POOL_WINDOWS = (2, 4, 8, 16)
N_GROUPS = 4
EXPERTS_PER_GROUP = 8
N_EXPERTS = N_GROUPS * EXPERTS_PER_GROUP
TOP_K = 2

LANES = 128
SUBLANES = 8
VMEM_CAP_BYTES = 56 * 1024 * 1024

ROUTER_ROWS = SUBLANES + N_EXPERTS
NEG_BIG = -1e30


def _vmem_limit(nbytes):
    return int(min(VMEM_CAP_BYTES, nbytes * 3 // 2 + (8 << 20)))


def _params(n_axes, nbytes):
    return pltpu.CompilerParams(
        dimension_semantics=("arbitrary",) * n_axes,
        vmem_limit_bytes=_vmem_limit(nbytes))


def _sigmoid(x):
    return 1.0 / (1.0 + jnp.exp(-x))


def _gelu_tanh(x):
    c = math.sqrt(2.0 / math.pi)
    return x * (0.5 * (1.0 + jnp.tanh(c * (x + 0.044715 * (x * x * x)))))


def _layer_norm(v, g, b):
    mu = jnp.mean(v, axis=-1, keepdims=True)
    vc = v - mu
    var = jnp.mean(vc * vc, axis=-1, keepdims=True)
    return vc * lax.rsqrt(var + EPS) * g + b


def _rms_norm(x, g):
    ms = jnp.mean(x * x, axis=-1, keepdims=True)
    return x * lax.rsqrt(ms + EPS) * g


def _norm_matmul_body(x_ref, g_ref, w_ref, o_ref, hn_ref):
    @pl.when(pl.program_id(1) == 0)
    def _():
        hn_ref[...] = _rms_norm(x_ref[...], g_ref[...]).astype(BF16)

    o_ref[...] = jnp.dot(hn_ref[...], w_ref[...], preferred_element_type=F32)


def _norm_matmul(x, g, w, *, tm, tn):
    t, d = x.shape
    n = w.shape[1]
    nbytes = 2 * (tm * d * 4 + d * tn * 2 + tm * tn * 4) + tm * d * 2
    return pl.pallas_call(
        _norm_matmul_body,
        grid=(t // tm, n // tn),
        in_specs=[pl.BlockSpec((tm, d), lambda i, j: (i, 0)),
                  pl.BlockSpec((1, d), lambda i, j: (0, 0)),
                  pl.BlockSpec((d, tn), lambda i, j: (0, j))],
        out_specs=pl.BlockSpec((tm, tn), lambda i, j: (i, j)),
        out_shape=jax.ShapeDtypeStruct((t, n), F32),
        scratch_shapes=[pltpu.VMEM((tm, d), BF16)],
        compiler_params=_params(2, nbytes),
        name="norm_matmul",
    )(x, g.reshape(1, d), w)


EVEN_HALO = SUBLANES


def _even_mixer_body(b_ref, c_ref, h_ref, u_ref, v_ref, cw_ref, lg_ref, lb_ref,
                     sw_ref, sb_ref, o_ref, pbuf):
    tm, c = b_ref.shape
    kw = cw_ref.shape[0]
    i = pl.program_id(0)

    @pl.when(i == 0)
    def _():
        pbuf[0:EVEN_HALO, :] = jnp.zeros((EVEN_HALO, c), F32)

    p = c_ref[...] * h_ref[...]
    pbuf[EVEN_HALO:EVEN_HALO + tm, :] = p
    conv = cw_ref[kw - 1:kw, :] * p
    for k in range(kw - 1):
        conv = conv + cw_ref[k:k + 1, :] * pbuf[pl.ds(EVEN_HALO - (kw - 1) + k, tm), :]
    o_ref[:, 0:c] = (b_ref[...] * conv).astype(BF16)
    pbuf[0:EVEN_HALO, :] = pbuf[tm:tm + EVEN_HALO, :]

    u = _gelu_tanh(u_ref[...])
    vn = _layer_norm(_gelu_tanh(v_ref[...]), lg_ref[...], lb_ref[...]).astype(BF16)
    tpos = lax.broadcasted_iota(I32, (SGU_BLOCK, SGU_BLOCK), 0)
    spos = lax.broadcasted_iota(I32, (SGU_BLOCK, SGU_BLOCK), 1)
    mask = (spos // CHUNK) <= (tpos // CHUNK)
    hd = c // SGU_HEADS
    for h in range(SGU_HEADS):
        wh = jnp.where(mask, sw_ref[h], 0.0).astype(BF16)
        bh = sb_ref[:, h:h + 1]
        cols = slice(h * hd, (h + 1) * hd)
        for n in range(tm // SGU_BLOCK):
            rows = slice(n * SGU_BLOCK, (n + 1) * SGU_BLOCK)
            sv = jnp.dot(wh, vn[rows, cols], preferred_element_type=F32) + bh
            o_ref[rows, c + h * hd:c + (h + 1) * hd] = (u[rows, cols] * sv).astype(BF16)


def _even_mixer(z, conv_w, ln_g, ln_b, sgu_w, sgu_b, *, tm):
    t = z.shape[0]
    c = conv_w.shape[1]
    zspec = lambda col: pl.BlockSpec((tm, c), lambda i, col=col: (i, col))
    const = lambda shape: pl.BlockSpec(shape, lambda i: (0,) * len(shape))
    nbytes = 2 * (5 * tm * c * 4 + tm * 2 * c * 2) + (tm + EVEN_HALO) * c * 4 + 8 * tm * c * 4
    return pl.pallas_call(
        _even_mixer_body,
        grid=(t // tm,),
        in_specs=[zspec(0), zspec(1), zspec(2), zspec(3), zspec(4),
                  const(conv_w.shape), const((1, c)), const((1, c)),
                  const(sgu_w.shape), const((SGU_BLOCK, SGU_HEADS))],
        out_specs=pl.BlockSpec((tm, 2 * c), lambda i: (i, 0)),
        out_shape=jax.ShapeDtypeStruct((t, 2 * c), BF16),
        scratch_shapes=[pltpu.VMEM((tm + EVEN_HALO, c), F32)],
        compiler_params=_params(1, nbytes),
        name="even_mixer",
    )(z, z, z, z, z, conv_w, ln_g.reshape(1, c), ln_b.reshape(1, c), sgu_w, sgu_b.T)


POOL_HALO = 16
CONF_HALO = 32
CONV_ROWS = 128


def _odd_mixer_body(x_ref, za_ref, zg_ref, pw_ref, ps_ref, dw_ref, db_ref, lg_ref, lb_ref,
                    o_ref, xbuf, gbuf, dbuf):
    tm, c = x_ref.shape
    i = pl.program_id(0)

    @pl.when(i == 0)
    def _():
        xbuf[0:POOL_HALO, :] = jnp.zeros((POOL_HALO, c), F32)
        gbuf[0:CONF_HALO, :] = jnp.zeros((CONF_HALO, c), F32)

    x = x_ref[...]
    xbuf[POOL_HALO:POOL_HALO + tm, :] = x
    step = i * tm + lax.broadcasted_iota(I32, (tm, 1), 0) + 1
    gd = c // len(POOL_WINDOWS)
    for g, w in enumerate(POOL_WINDOWS):
        cols = slice(g * gd, (g + 1) * gd)
        xg = x[:, cols]
        s = xg
        for j in range(1, w):
            s = s + xbuf[pl.ds(POOL_HALO - j, tm), cols]
        inv_count = 1.0 / jnp.minimum(step, w).astype(F32)
        d = s * inv_count - xg
        y = jnp.dot(d.astype(BF16), pw_ref[g].astype(BF16), preferred_element_type=F32)
        o_ref[:, cols] = (y * ps_ref[:, cols]).astype(BF16)
    xbuf[0:POOL_HALO, :] = xbuf[tm:tm + POOL_HALO, :]

    kw = dw_ref.shape[0]
    gbuf[CONF_HALO:CONF_HALO + tm, :] = za_ref[...] * _sigmoid(zg_ref[...])

    def conv_cols(cb, carry):
        c0 = pl.multiple_of(cb * LANES, LANES)
        for r0 in range(0, tm, CONV_ROWS):
            acc = jnp.broadcast_to(db_ref[:, pl.ds(c0, LANES)], (CONV_ROWS, LANES))
            for k in range(kw):
                tap = gbuf[pl.ds(r0 + CONF_HALO - (kw - 1) + k, CONV_ROWS), pl.ds(c0, LANES)]
                acc = acc + dw_ref[k:k + 1, pl.ds(c0, LANES)] * tap
            dbuf[r0:r0 + CONV_ROWS, pl.ds(c0, LANES)] = acc
        return carry

    lax.fori_loop(0, c // LANES, conv_cols, 0)
    gbuf[0:CONF_HALO, :] = gbuf[tm:tm + CONF_HALO, :]
    yn = _layer_norm(dbuf[...], lg_ref[...], lb_ref[...])
    o_ref[:, c:2 * c] = (yn * _sigmoid(yn)).astype(BF16)


def _odd_mixer(z, pool_w, pool_scale, dw_w, dw_b, ln_g, ln_b, *, tm):
    t = z.shape[0]
    c = dw_w.shape[1]
    zspec = lambda col: pl.BlockSpec((tm, c), lambda i, col=col: (i, col))
    const = lambda shape: pl.BlockSpec(shape, lambda i: (0,) * len(shape))
    nbytes = (2 * (3 * tm * c * 4 + tm * 2 * c * 2) + (3 * tm + POOL_HALO + CONF_HALO) * c * 4
              + 6 * tm * c * 4)
    return pl.pallas_call(
        _odd_mixer_body,
        grid=(t // tm,),
        in_specs=[zspec(0), zspec(1), zspec(2),
                  const(pool_w.shape), const((1, c)), const(dw_w.shape), const((1, c)),
                  const((1, c)), const((1, c))],
        out_specs=pl.BlockSpec((tm, 2 * c), lambda i: (i, 0)),
        out_shape=jax.ShapeDtypeStruct((t, 2 * c), BF16),
        scratch_shapes=[pltpu.VMEM((tm + POOL_HALO, c), F32),
                        pltpu.VMEM((tm + CONF_HALO, c), F32),
                        pltpu.VMEM((tm, c), F32)],
        compiler_params=_params(1, nbytes),
        name="odd_mixer",
    )(z, z, z, pool_w, pool_scale.reshape(1, c), dw_w, dw_b.reshape(1, c),
      ln_g.reshape(1, c), ln_b.reshape(1, c))


def _split_bf16(v):
    head = v.astype(BF16)
    return head, (v - head.astype(F32)).astype(BF16)


def _proj_body(y_ref, w_ref, x_ref, g_ref, wh_ref, wl_ref, br_ref, x1_ref, hn_ref, lg_ref):
    x1 = x_ref[...] + jnp.dot(y_ref[...], w_ref[...], preferred_element_type=F32)
    x1_ref[...] = x1
    hn = _rms_norm(x1, g_ref[...])
    hn_ref[...] = hn
    head, rem = _split_bf16(hn)
    lg_ref[...] = (jnp.dot(head, wh_ref[...], preferred_element_type=F32)
                   + jnp.dot(rem, wh_ref[...], preferred_element_type=F32)
                   + jnp.dot(head, wl_ref[...], preferred_element_type=F32)
                   + br_ref[...])


def _proj(ycat, w_out, x, g, wr, br, *, tm):
    t, d = x.shape
    k = ycat.shape[1]
    wh, wl = _split_bf16(wr)
    const = lambda shape: pl.BlockSpec(shape, lambda i: (0,) * len(shape))
    nbytes = (2 * (tm * k * 2 + k * d * 2 + 3 * tm * d * 4 + 2 * d * LANES * 2)
              + 5 * tm * d * 4)
    return pl.pallas_call(
        _proj_body,
        grid=(t // tm,),
        in_specs=[pl.BlockSpec((tm, k), lambda i: (i, 0)), const((k, d)),
                  pl.BlockSpec((tm, d), lambda i: (i, 0)), const((1, d)),
                  const((d, LANES)), const((d, LANES)), const((1, LANES))],
        out_specs=[pl.BlockSpec((tm, d), lambda i: (i, 0)),
                   pl.BlockSpec((tm, d), lambda i: (i, 0)),
                   pl.BlockSpec((tm, LANES), lambda i: (i, 0))],
        out_shape=[jax.ShapeDtypeStruct((t, d), F32),
                   jax.ShapeDtypeStruct((t, d), F32),
                   jax.ShapeDtypeStruct((t, LANES), F32)],
        compiler_params=_params(1, nbytes),
        name="proj",
    )(ycat, w_out, x, g.reshape(1, d), wh, wl, br)


RANK_BLOCK = 256


def _route_block(logits, run, before):
    n = logits.shape[1]
    row8 = lax.broadcasted_iota(I32, (SUBLANES, n), 0)
    row = lax.broadcasted_iota(I32, (N_EXPERTS, n), 0)

    gl = jnp.where(row8 < N_GROUPS, logits[0:SUBLANES], NEG_BIG)
    gmax = jnp.max(gl, axis=0, keepdims=True)
    gidx = jnp.min(jnp.where(gl == gmax, row8, N_EXPERTS), axis=0, keepdims=True)
    p_group = 1.0 / jnp.sum(jnp.exp(gl - gmax), axis=0, keepdims=True)

    el = jnp.where(row // EXPERTS_PER_GROUP == gidx, logits[SUBLANES:ROUTER_ROWS], NEG_BIG)
    m0 = jnp.max(el, axis=0, keepdims=True)
    e0 = jnp.min(jnp.where(el == m0, row, N_EXPERTS), axis=0, keepdims=True)
    el1 = jnp.where(row == e0, NEG_BIG, el)
    m1 = jnp.max(el1, axis=0, keepdims=True)
    e1 = jnp.min(jnp.where(el1 == m1, row, N_EXPERTS), axis=0, keepdims=True)
    ratio = jnp.exp(m1 - m0)
    gate0 = p_group / (1.0 + ratio)
    gate1 = p_group * ratio / (1.0 + ratio)

    sel0 = row == e0
    sel1 = row == e1
    pre0 = jnp.dot(sel0.astype(BF16), before, preferred_element_type=F32)
    pre1 = jnp.dot(sel1.astype(BF16), before, preferred_element_type=F32)
    cnt0 = jnp.sum(sel0.astype(F32), axis=1, keepdims=True)
    cnt1 = jnp.sum(sel1.astype(F32), axis=1, keepdims=True)
    rank0 = jnp.sum(jnp.where(sel0, pre0 + run, 0.0), axis=0, keepdims=True)
    rank1 = jnp.sum(jnp.where(sel1, pre1 + (run + cnt0), 0.0), axis=0, keepdims=True)

    meta = jnp.where(row8 == 0, e0, jnp.where(row8 == 1, e1, jnp.where(
        row8 == 2, rank0.astype(I32), jnp.where(row8 == 3, rank1.astype(I32), 0))))
    gates = jnp.where(row8 == 0, gate0, jnp.where(row8 == 1, gate1, 0.0))
    return meta, gates, run + (cnt0 + cnt1)


def _route_body(lg_ref, meta_ref, gate_ref, cnt_ref, run_ref):
    tr = lg_ref.shape[0]
    nb = min(RANK_BLOCK, tr)

    @pl.when(pl.program_id(0) == 0)
    def _():
        run_ref[...] = jnp.zeros(run_ref.shape, F32)

    src = lax.broadcasted_iota(I32, (nb, nb), 0)
    dst = lax.broadcasted_iota(I32, (nb, nb), 1)
    before = (src < dst).astype(BF16)
    run = run_ref[:, 0:1]
    for b in range(tr // nb):
        cols = slice(b * nb, (b + 1) * nb)
        logits_t = lg_ref[cols, :].T[0:ROUTER_ROWS]
        meta, gates, run = _route_block(logits_t, run, before)
        meta_ref[:, cols] = meta
        gate_ref[:, cols] = gates
    run_ref[...] = jnp.broadcast_to(run, run_ref.shape)
    cnt_ref[...] = run_ref[...]


def _route(logits, *, tr):
    t = logits.shape[0]
    return pl.pallas_call(
        _route_body,
        grid=(t // tr,),
        in_specs=[pl.BlockSpec((tr, LANES), lambda i: (i, 0))],
        out_specs=[pl.BlockSpec((SUBLANES, tr), lambda i: (0, i)),
                   pl.BlockSpec((SUBLANES, tr), lambda i: (0, i)),
                   pl.BlockSpec((N_EXPERTS, LANES), lambda i: (0, 0))],
        out_shape=[jax.ShapeDtypeStruct((SUBLANES, t), I32),
                   jax.ShapeDtypeStruct((SUBLANES, t), F32),
                   jax.ShapeDtypeStruct((N_EXPERTS, LANES), F32)],
        scratch_shapes=[pltpu.VMEM((N_EXPERTS, LANES), F32)],
        compiler_params=_params(1, 64 * tr * 4 * 4),
        name="route",
    )(logits)


ISSUE_UNROLL = 8


def _wait_rows(src_ref, dst_ref, sem):
    pltpu.make_async_copy(src_ref, dst_ref, sem).wait()


def _scatter_body(pos_ref, h_ref, xs_ref, sem):
    tm = h_ref.shape[0]

    def issue(r, carry):
        for k in range(TOP_K):
            p = pos_ref[0, k, r]
            pltpu.make_async_copy(h_ref.at[pl.ds(r, 1), :], xs_ref.at[pl.ds(p, 1), :], sem).start()
        return carry

    lax.fori_loop(0, tm, issue, 0, unroll=ISSUE_UNROLL)
    for k in range(TOP_K):
        _wait_rows(h_ref, xs_ref.at[pl.ds(0, tm), :], sem)


def _scatter_rows(hn, pos3, *, tm):
    t, d = hn.shape
    return pl.pallas_call(
        _scatter_body,
        grid=(t // tm,),
        in_specs=[pl.BlockSpec((1, TOP_K, tm), lambda i: (i, 0, 0), memory_space=pltpu.SMEM),
                  pl.BlockSpec((tm, d), lambda i: (i, 0))],
        out_specs=pl.BlockSpec(memory_space=pl.ANY),
        out_shape=jax.ShapeDtypeStruct((TOP_K * t, d), hn.dtype),
        scratch_shapes=[pltpu.SemaphoreType.DMA],
        compiler_params=_params(1, 2 * tm * d * 4),
        name="scatter_rows",
    )(pos3, hn)


def _expert_body(vblk, vexp, vfirst, vnew, vvalid, estart,
                 x_ref, w1_ref, w3_ref, w2_ref, o_ref, w1b, w3b, w2b):
    bm = x_ref.shape[0]
    v = pl.program_id(0)

    @pl.when(vvalid[v] == 1)
    def _():
        @pl.when(vnew[v] == 1)
        def _():
            w1b[...] = w1_ref[0, 0].astype(BF16)
            w3b[...] = w3_ref[0, 0].astype(BF16)
            w2b[...] = w2_ref[0, 0].astype(BF16)

        xb = x_ref[...].astype(BF16)
        h1 = jnp.dot(xb, w1b[...], preferred_element_type=F32)
        h3 = jnp.dot(xb, w3b[...], preferred_element_type=F32)
        hid = (h1 * _sigmoid(h1)) * h3
        y = jnp.dot(hid.astype(BF16), w2b[...], preferred_element_type=F32)

        @pl.when(vfirst[v] == 1)
        def _():
            o_ref[...] = y

        @pl.when(vfirst[v] == 0)
        def _():
            e = vexp[v]
            r = vblk[v] * bm + lax.broadcasted_iota(I32, (bm, 1), 0)
            mine = (r >= estart[e]) & (r < estart[e + 1])
            o_ref[...] = jnp.where(mine, y, o_ref[...])


def _experts(xs, w1, w3, w2, tables, *, layer, bm):
    n, d = xs.shape
    f = w1.shape[3]
    n_visits = tables[0].shape[0]
    nbytes = 2 * (bm * d * 4 * 2 + 3 * d * f * 4) + 3 * d * f * 2 + 6 * bm * d * 4
    wmap = lambda v, blk, exp, *_: (layer, exp[v], 0, 0)
    grid_spec = pltpu.PrefetchScalarGridSpec(
        num_scalar_prefetch=len(tables),
        grid=(n_visits,),
        in_specs=[pl.BlockSpec((bm, d), lambda v, blk, exp, *_: (blk[v], 0)),
                  pl.BlockSpec((1, 1, d, f), wmap),
                  pl.BlockSpec((1, 1, d, f), wmap),
                  pl.BlockSpec((1, 1, f, d), wmap)],
        out_specs=pl.BlockSpec((bm, d), lambda v, blk, exp, *_: (blk[v], 0)),
        scratch_shapes=[pltpu.VMEM((d, f), BF16), pltpu.VMEM((d, f), BF16),
                        pltpu.VMEM((f, d), BF16)])
    return pl.pallas_call(
        _expert_body,
        grid_spec=grid_spec,
        out_shape=jax.ShapeDtypeStruct((n, d), F32),
        compiler_params=_params(1, nbytes),
        name="experts",
    )(*tables, xs, w1, w3, w2)


def _visit_tables(counts, n_slots, bm):
    n_blocks = n_slots // bm
    n_visits = n_blocks + N_EXPERTS - 1
    end = jnp.cumsum(counts)
    start = end - counts
    first_blk = start // bm
    last_blk = jnp.maximum(end - 1, 0) // bm
    per_e = jnp.where(counts > 0, last_blk - first_blk + 1, 0)
    cum = jnp.cumsum(per_e)
    total = cum[-1]
    v = jnp.minimum(jnp.arange(n_visits, dtype=I32), total - 1)
    e = jnp.sum((cum[None, :] <= v[:, None]).astype(I32), axis=1)
    blk = (first_blk[e] + (v - (cum[e] - per_e[e]))).astype(I32)
    valid = (jnp.arange(n_visits) < total).astype(I32)
    prev_blk = jnp.concatenate([jnp.full((1,), -1, I32), blk[:-1]])
    prev_e = jnp.concatenate([jnp.full((1,), -1, I32), e[:-1]])
    first = (blk != prev_blk).astype(I32)
    new = (e != prev_e).astype(I32)
    estart = jnp.concatenate([start, end[-1:]]).astype(I32)
    return blk, e, first, new, valid, estart


def _combine_body(pos_ref, x_ref, gt_ref, g_ref, ys_ref, o_ref, ybuf, sem, *, final_norm):
    tm = x_ref.shape[0]

    def issue(r, carry):
        for k in range(TOP_K):
            p = pos_ref[0, k, r]
            pltpu.make_async_copy(ys_ref.at[pl.ds(p, 1), :], ybuf.at[k, pl.ds(r, 1), :], sem).start()
        return carry

    lax.fori_loop(0, tm, issue, 0, unroll=ISSUE_UNROLL)
    for k in range(TOP_K):
        _wait_rows(ys_ref.at[pl.ds(0, tm), :], ybuf.at[k], sem)
    x2 = x_ref[...] + gt_ref[:, 0:1] * ybuf[0] + gt_ref[:, 1:2] * ybuf[1]
    if final_norm:
        x2 = _rms_norm(x2, g_ref[...])
    o_ref[...] = x2


def _combine(x1, ys, pos3, gates_t, g_final, *, tm, final_norm):
    t, d = x1.shape
    nbytes = 2 * (2 * tm * d * 4 + tm * LANES * 4) + TOP_K * tm * d * 4 + 2 * tm * d * 4
    return pl.pallas_call(
        functools.partial(_combine_body, final_norm=final_norm),
        grid=(t // tm,),
        in_specs=[pl.BlockSpec((1, TOP_K, tm), lambda i: (i, 0, 0), memory_space=pltpu.SMEM),
                  pl.BlockSpec((tm, d), lambda i: (i, 0)),
                  pl.BlockSpec((tm, TOP_K), lambda i: (i, 0)),
                  pl.BlockSpec((1, d), lambda i: (0, 0)),
                  pl.BlockSpec(memory_space=pl.ANY)],
        out_specs=pl.BlockSpec((tm, d), lambda i: (i, 0)),
        out_shape=jax.ShapeDtypeStruct((t, d), F32),
        scratch_shapes=[pltpu.VMEM((TOP_K, tm, d), F32), pltpu.SemaphoreType.DMA],
        compiler_params=_params(1, nbytes),
        name="combine",
    )(pos3, x1, gates_t, g_final.reshape(1, d), ys)


def _tiles(t):
    return dict(tm_in=min(512, t), tn_in=1024, tm_mix=min(256, t), tm_proj=min(256, t), tr=min(2048, t),
                tm_rows=min(256, t), bm=min(256, TOP_K * t))


def _moe(ycat, w_out, x, ln_g, w_group, b_group, w_router, b_router, w1, w3, w2, layer,
         g_final, final_norm, tl):
    t, d = x.shape
    gpad = SUBLANES - N_GROUPS
    wr = jnp.concatenate([w_group, jnp.zeros((d, gpad), F32), w_router,
                          jnp.zeros((d, LANES - ROUTER_ROWS), F32)], axis=1)
    br = jnp.concatenate([b_group, jnp.zeros((gpad,), F32), b_router,
                          jnp.zeros((LANES - ROUTER_ROWS,), F32)]).reshape(1, LANES)
    x1, hn, logits = _proj(ycat, w_out, x, ln_g, wr, br, tm=tl["tm_proj"])
    meta, gates, cnt = _route(logits, tr=tl["tr"])

    counts = cnt[:, 0].astype(I32)
    start = jnp.cumsum(counts) - counts
    is_e = meta[0:TOP_K, :, None] == jnp.arange(N_EXPERTS, dtype=I32)
    pos = jnp.sum(jnp.where(is_e, start, 0), axis=-1) + meta[TOP_K:2 * TOP_K]
    tm = tl["tm_rows"]
    pos3 = pos.reshape(TOP_K, t // tm, tm).transpose(1, 0, 2)

    xs = _scatter_rows(hn, pos3, tm=tm)
    tables = _visit_tables(counts, TOP_K * t, tl["bm"])
    ys = _experts(xs, w1, w3, w2, tables, layer=layer, bm=tl["bm"])
    return _combine(x1, ys, pos3, gates[0:TOP_K].T, g_final, tm=tm, final_norm=final_norm)


def kernel(x, ln_mix, ln_ffn, ln_final, ev_w_in, ev_conv_w, ev_sgu_ln_g, ev_sgu_ln_b, ev_sgu_w, ev_sgu_b, ev_w_out, od_w_in, od_pool_w, od_pool_scale, od_dw_w, od_dw_b, od_ln_g, od_ln_b, od_w_out, moe_w_group, moe_b_group, moe_w_router, moe_b_router, moe_w1, moe_w3, moe_w2):
    bsz, s, d = x.shape
    assert bsz == 1, "token mixers carry state across row tiles of one sequence"
    depth = ln_mix.shape[0]
    tl = _tiles(s)
    xf = x.reshape(s, d)
    for layer in range(depth):
        i = layer // 2
        last = layer == depth - 1
        if layer % 2 == 0:
            z = _norm_matmul(xf, ln_mix[layer], ev_w_in[i].astype(BF16), tm=tl["tm_in"], tn=tl["tn_in"])
            ycat = _even_mixer(z, ev_conv_w[i], ev_sgu_ln_g[i], ev_sgu_ln_b[i], ev_sgu_w[i],
                               ev_sgu_b[i], tm=tl["tm_mix"])
            w_out = ev_w_out[i]
        else:
            z = _norm_matmul(xf, ln_mix[layer], od_w_in[i].astype(BF16), tm=tl["tm_in"], tn=tl["tn_in"])
            ycat = _odd_mixer(z, od_pool_w[i], od_pool_scale[i], od_dw_w[i], od_dw_b[i],
                              od_ln_g[i], od_ln_b[i], tm=tl["tm_mix"])
            w_out = od_w_out[i]
        xf = _moe(ycat, w_out.astype(BF16), xf, ln_ffn[layer], moe_w_group[layer], moe_b_group[layer],
                  moe_w_router[layer], moe_b_router[layer], moe_w1, moe_w3, moe_w2, layer,
                  ln_final, last, tl)
    return xf.reshape(bsz, s, d)
```

```python
import functools
import math

import jax
import jax.numpy as jnp
from jax import lax
from jax.experimental import pallas as pl
from jax.experimental.pallas import tpu as pltpu

F32 = jnp.float32
BF16 = jnp.bfloat16
I32 = jnp.int32

EPS = 1e-6
CHUNK = 64
SGU_BLOCK = 128
SGU_HEADS = 8
POOL_WINDOWS = (2, 4, 8, 16)
N_GROUPS = 4
EXPERTS_PER_GROUP = 8
N_EXPERTS = N_GROUPS * EXPERTS_PER_GROUP
TOP_K = 2

LANES = 128
SUBLANES = 8
VMEM_CAP_BYTES = 60 * 1024 * 1024

ROUTER_ROWS = SUBLANES + N_EXPERTS
NEG_BIG = -1e30


def _vmem_limit(nbytes):
    return int(min(VMEM_CAP_BYTES, nbytes * 5 // 4 + (6 << 20)))


def _params(n_axes, nbytes):
    return pltpu.CompilerParams(
        dimension_semantics=("arbitrary",) * n_axes,
        vmem_limit_bytes=_vmem_limit(nbytes))


def _const_spec(shape, single_buffer=False):
    mode = dict(pipeline_mode=pl.Buffered(1)) if single_buffer else {}
    return pl.BlockSpec(shape, lambda i: (0,) * len(shape), **mode)


def _sigmoid(x):
    return 1.0 / (1.0 + jnp.exp(-x))


def _gelu_tanh(x):
    c = math.sqrt(2.0 / math.pi)
    return x * (0.5 * (1.0 + jnp.tanh(c * (x + 0.044715 * (x * x * x)))))


def _layer_norm(v, g, b):
    mu = jnp.mean(v, axis=-1, keepdims=True)
    vc = v - mu
    var = jnp.mean(vc * vc, axis=-1, keepdims=True)
    return vc * lax.rsqrt(var + EPS) * g + b


def _rms_norm(x, g):
    ms = jnp.mean(x * x, axis=-1, keepdims=True)
    return x * lax.rsqrt(ms + EPS) * g


def _split_bf16(v):
    head = v.astype(BF16)
    return head, (v - head.astype(F32)).astype(BF16)


IN_PROJ_COLS = 1024


def _in_proj(x_ref, g_ref, w_ref, z_ref):
    h = _rms_norm(x_ref[...], g_ref[...]).astype(BF16)
    for c0 in range(0, w_ref.shape[1], IN_PROJ_COLS):
        cols = slice(c0, c0 + IN_PROJ_COLS)
        z_ref[:, cols] = jnp.dot(h, w_ref[:, cols], preferred_element_type=F32)


def _out_proj_route(x_ref, y_ref, w_ref, g_ref, wh_ref, wl_ref, br_ref, x1_ref, hn_ref, lg_ref):
    x1 = x_ref[...] + jnp.dot(y_ref[...], w_ref[...], preferred_element_type=F32)
    x1_ref[...] = x1
    hn = _rms_norm(x1, g_ref[...])
    hn_ref[...] = hn
    head, rem = _split_bf16(hn)
    lg_ref[...] = (jnp.dot(head, wh_ref[...], preferred_element_type=F32)
                   + jnp.dot(rem, wh_ref[...], preferred_element_type=F32)
                   + jnp.dot(head, wl_ref[...], preferred_element_type=F32)
                   + br_ref[...])


EVEN_HALO = SUBLANES


def _even_mix(z_ref, cw_ref, lg_ref, lb_ref, sw_ref, sb_ref, o_ref, pbuf):
    tm = z_ref.shape[0]
    kw, c = cw_ref.shape
    col = lambda j: z_ref[:, j * c:(j + 1) * c]

    @pl.when(pl.program_id(0) == 0)
    def _():
        pbuf[0:EVEN_HALO, :] = jnp.zeros((EVEN_HALO, c), F32)

    p = col(1) * col(2)
    pbuf[EVEN_HALO:EVEN_HALO + tm, :] = p
    conv = cw_ref[kw - 1:kw, :] * p
    for k in range(kw - 1):
        conv = conv + cw_ref[k:k + 1, :] * pbuf[pl.ds(EVEN_HALO - (kw - 1) + k, tm), :]
    o_ref[:, 0:c] = (col(0) * conv).astype(BF16)
    pbuf[0:EVEN_HALO, :] = pbuf[tm:tm + EVEN_HALO, :]

    u = _gelu_tanh(col(3))
    vn = _layer_norm(_gelu_tanh(col(4)), lg_ref[...], lb_ref[...]).astype(BF16)
    tpos = lax.broadcasted_iota(I32, (SGU_BLOCK, SGU_BLOCK), 0)
    spos = lax.broadcasted_iota(I32, (SGU_BLOCK, SGU_BLOCK), 1)
    mask = (spos // CHUNK) <= (tpos // CHUNK)
    hd = c // SGU_HEADS
    for h in range(SGU_HEADS):
        wh = jnp.where(mask, sw_ref[h], 0.0).astype(BF16)
        bh = sb_ref[:, h:h + 1]
        cols = slice(h * hd, (h + 1) * hd)
        for n in range(tm // SGU_BLOCK):
            rows = slice(n * SGU_BLOCK, (n + 1) * SGU_BLOCK)
            sv = jnp.dot(wh, vn[rows, cols], preferred_element_type=F32) + bh
            o_ref[rows, c + h * hd:c + (h + 1) * hd] = (u[rows, cols] * sv).astype(BF16)


POOL_HALO = 16
CONF_HALO = 32
CONV_ROWS = 128


def _odd_mix(z_ref, pw_ref, ps_ref, dw_ref, db_ref, lg_ref, lb_ref, o_ref, xbuf, gbuf, dbuf):
    tm = z_ref.shape[0]
    kw, c = dw_ref.shape
    i = pl.program_id(0)

    @pl.when(i == 0)
    def _():
        xbuf[0:POOL_HALO, :] = jnp.zeros((POOL_HALO, c), F32)
        gbuf[0:CONF_HALO, :] = jnp.zeros((CONF_HALO, c), F32)

    x = z_ref[:, 0:c]
    xbuf[POOL_HALO:POOL_HALO + tm, :] = x
    step = i * tm + lax.broadcasted_iota(I32, (tm, 1), 0) + 1
    gd = c // len(POOL_WINDOWS)
    for g, w in enumerate(POOL_WINDOWS):
        cols = slice(g * gd, (g + 1) * gd)
        xg = x[:, cols]
        s = xg
        for j in range(1, w):
            s = s + xbuf[pl.ds(POOL_HALO - j, tm), cols]
        inv_count = 1.0 / jnp.minimum(step, w).astype(F32)
        d = s * inv_count - xg
        y = jnp.dot(d.astype(BF16), pw_ref[g].astype(BF16), preferred_element_type=F32)
        o_ref[:, cols] = (y * ps_ref[:, cols]).astype(BF16)
    xbuf[0:POOL_HALO, :] = xbuf[tm:tm + POOL_HALO, :]

    gbuf[CONF_HALO:CONF_HALO + tm, :] = z_ref[:, c:2 * c] * _sigmoid(z_ref[:, 2 * c:3 * c])

    def conv_cols(cb, carry):
        c0 = pl.multiple_of(cb * LANES, LANES)
        for r0 in range(0, tm, CONV_ROWS):
            acc = jnp.broadcast_to(db_ref[:, pl.ds(c0, LANES)], (CONV_ROWS, LANES))
            for k in range(kw):
                tap = gbuf[pl.ds(r0 + CONF_HALO - (kw - 1) + k, CONV_ROWS), pl.ds(c0, LANES)]
                acc = acc + dw_ref[k:k + 1, pl.ds(c0, LANES)] * tap
            dbuf[r0:r0 + CONV_ROWS, pl.ds(c0, LANES)] = acc
        return carry

    lax.fori_loop(0, c // LANES, conv_cols, 0)
    gbuf[0:CONF_HALO, :] = gbuf[tm:tm + CONF_HALO, :]
    yn = _layer_norm(dbuf[...], lg_ref[...], lb_ref[...])
    o_ref[:, c:2 * c] = (yn * _sigmoid(yn)).astype(BF16)


def _even_layer_body(x_ref, gm_ref, win_ref, cw_ref, lg_ref, lb_ref, sw_ref, sb_ref,
                     wout_ref, gf_ref, wh_ref, wl_ref, br_ref,
                     x1_ref, hn_ref, lgt_ref, z_ref, y_ref, pbuf):
    _in_proj(x_ref, gm_ref, win_ref, z_ref)
    _even_mix(z_ref, cw_ref, lg_ref, lb_ref, sw_ref, sb_ref, y_ref, pbuf)
    _out_proj_route(x_ref, y_ref, wout_ref, gf_ref, wh_ref, wl_ref, br_ref, x1_ref, hn_ref, lgt_ref)


def _odd_layer_body(x_ref, gm_ref, win_ref, pw_ref, ps_ref, dw_ref, db_ref, lg_ref, lb_ref,
                    wout_ref, gf_ref, wh_ref, wl_ref, br_ref,
                    x1_ref, hn_ref, lgt_ref, z_ref, y_ref, xbuf, gbuf, dbuf):
    _in_proj(x_ref, gm_ref, win_ref, z_ref)
    _odd_mix(z_ref, pw_ref, ps_ref, dw_ref, db_ref, lg_ref, lb_ref, y_ref, xbuf, gbuf, dbuf)
    _out_proj_route(x_ref, y_ref, wout_ref, gf_ref, wh_ref, wl_ref, br_ref, x1_ref, hn_ref, lgt_ref)


def _layer_front(body, x, g_mix, w_in, mix_params, mix_scratch, w_out, g_ffn, wr, br, *, tm):
    t, d = x.shape
    n = w_in.shape[1]
    c2 = w_out.shape[0]
    wh, wl = _split_bf16(wr)
    row = lambda width: pl.BlockSpec((tm, width), lambda i: (i, 0))
    small = [g_mix.reshape(1, d)] , mix_params, [g_ffn.reshape(1, d), wh, wl, br]
    scratch = [pltpu.VMEM((tm, n), F32), pltpu.VMEM((tm, c2), BF16)] + mix_scratch
    nbytes = (d * n * 2 + c2 * d * 2
              + 2 * 3 * tm * d * 4 + 2 * tm * LANES * 4
              + tm * n * 4 + tm * c2 * 2
              + sum(math.prod(s.shape) * 4 for s in mix_scratch)
              + 2 * sum(p.size * p.dtype.itemsize for p in mix_params + small[2][1:])
              + 8 * tm * d * 4)
    return pl.pallas_call(
        body,
        grid=(t // tm,),
        in_specs=([row(d), _const_spec((1, d)), _const_spec(w_in.shape, single_buffer=True)]
                  + [_const_spec(p.shape) for p in mix_params]
                  + [_const_spec(w_out.shape, single_buffer=True), _const_spec((1, d)),
                     _const_spec(wh.shape), _const_spec(wl.shape), _const_spec(br.shape)]),
        out_specs=[row(d), row(d), row(LANES)],
        out_shape=[jax.ShapeDtypeStruct((t, d), F32),
                   jax.ShapeDtypeStruct((t, d), F32),
                   jax.ShapeDtypeStruct((t, LANES), F32)],
        scratch_shapes=scratch,
        compiler_params=_params(1, nbytes),
        name=body.__name__.strip("_").replace("_body", ""),
    )(x, small[0][0], w_in, *mix_params, w_out, *small[2])


RANK_BLOCK = 256


def _route_block(logits, run, before):
    n = logits.shape[1]
    row8 = lax.broadcasted_iota(I32, (SUBLANES, n), 0)
    row = lax.broadcasted_iota(I32, (N_EXPERTS, n), 0)

    gl = jnp.where(row8 < N_GROUPS, logits[0:SUBLANES], NEG_BIG)
    gmax = jnp.max(gl, axis=0, keepdims=True)
    gidx = jnp.min(jnp.where(gl == gmax, row8, N_EXPERTS), axis=0, keepdims=True)
    p_group = 1.0 / jnp.sum(jnp.exp(gl - gmax), axis=0, keepdims=True)

    el = jnp.where(row // EXPERTS_PER_GROUP == gidx, logits[SUBLANES:ROUTER_ROWS], NEG_BIG)
    m0 = jnp.max(el, axis=0, keepdims=True)
    e0 = jnp.min(jnp.where(el == m0, row, N_EXPERTS), axis=0, keepdims=True)
    el1 = jnp.where(row == e0, NEG_BIG, el)
    m1 = jnp.max(el1, axis=0, keepdims=True)
    e1 = jnp.min(jnp.where(el1 == m1, row, N_EXPERTS), axis=0, keepdims=True)
    ratio = jnp.exp(m1 - m0)
    gate0 = p_group / (1.0 + ratio)
    gate1 = p_group * ratio / (1.0 + ratio)

    sel0 = row == e0
    sel1 = row == e1
    pre0 = jnp.dot(sel0.astype(BF16), before, preferred_element_type=F32)
    pre1 = jnp.dot(sel1.astype(BF16), before, preferred_element_type=F32)
    cnt0 = jnp.sum(sel0.astype(F32), axis=1, keepdims=True)
    cnt1 = jnp.sum(sel1.astype(F32), axis=1, keepdims=True)
    rank0 = jnp.sum(jnp.where(sel0, pre0 + run, 0.0), axis=0, keepdims=True)
    rank1 = jnp.sum(jnp.where(sel1, pre1 + (run + cnt0), 0.0), axis=0, keepdims=True)

    meta = jnp.where(row8 == 0, e0, jnp.where(row8 == 1, e1, jnp.where(
        row8 == 2, rank0.astype(I32), jnp.where(row8 == 3, rank1.astype(I32), 0))))
    gates = jnp.where(row8 == 0, gate0, jnp.where(row8 == 1, gate1, 0.0))
    return meta, gates, run + (cnt0 + cnt1)


def _route_body(lg_ref, meta_ref, gate_ref, cnt_ref, run_ref):
    tr = lg_ref.shape[0]
    nb = min(RANK_BLOCK, tr)

    @pl.when(pl.program_id(0) == 0)
    def _():
        run_ref[...] = jnp.zeros(run_ref.shape, F32)

    src = lax.broadcasted_iota(I32, (nb, nb), 0)
    dst = lax.broadcasted_iota(I32, (nb, nb), 1)
    before = (src < dst).astype(BF16)
    run = run_ref[:, 0:1]
    for b in range(tr // nb):
        cols = slice(b * nb, (b + 1) * nb)
        logits_t = lg_ref[cols, :].T[0:ROUTER_ROWS]
        meta, gates, run = _route_block(logits_t, run, before)
        meta_ref[:, cols] = meta
        gate_ref[:, cols] = gates
    run_ref[...] = jnp.broadcast_to(run, run_ref.shape)
    cnt_ref[...] = run_ref[...]


def _route(logits, *, tr):
    t = logits.shape[0]
    return pl.pallas_call(
        _route_body,
        grid=(t // tr,),
        in_specs=[pl.BlockSpec((tr, LANES), lambda i: (i, 0))],
        out_specs=[pl.BlockSpec((SUBLANES, tr), lambda i: (0, i)),
                   pl.BlockSpec((SUBLANES, tr), lambda i: (0, i)),
                   pl.BlockSpec((N_EXPERTS, LANES), lambda i: (0, 0))],
        out_shape=[jax.ShapeDtypeStruct((SUBLANES, t), I32),
                   jax.ShapeDtypeStruct((SUBLANES, t), F32),
                   jax.ShapeDtypeStruct((N_EXPERTS, LANES), F32)],
        scratch_shapes=[pltpu.VMEM((N_EXPERTS, LANES), F32)],
        compiler_params=_params(1, 64 * tr * 4 * 4),
        name="route",
    )(logits)


ISSUE_UNROLL = 8


def _pos_spec(tm):
    return pl.BlockSpec((1, 1, TOP_K * tm), lambda i: (i, 0, 0), memory_space=pltpu.SMEM)


def _wait_rows(src_ref, dst_ref, sem):
    pltpu.make_async_copy(src_ref, dst_ref, sem).wait()


def _scatter_body(pos_ref, h_ref, xs_ref, sem):
    tm = h_ref.shape[0]

    def issue(r, carry):
        for k in range(TOP_K):
            p = pos_ref[0, 0, k * tm + r]
            pltpu.make_async_copy(h_ref.at[pl.ds(r, 1), :], xs_ref.at[pl.ds(p, 1), :], sem).start()
        return carry

    lax.fori_loop(0, tm, issue, 0, unroll=ISSUE_UNROLL)
    for k in range(TOP_K):
        _wait_rows(h_ref, xs_ref.at[pl.ds(0, tm), :], sem)


def _scatter_rows(hn, pos3, *, tm):
    t, d = hn.shape
    return pl.pallas_call(
        _scatter_body,
        grid=(t // tm,),
        in_specs=[_pos_spec(tm), pl.BlockSpec((tm, d), lambda i: (i, 0))],
        out_specs=pl.BlockSpec(memory_space=pl.ANY),
        out_shape=jax.ShapeDtypeStruct((TOP_K * t, d), hn.dtype),
        scratch_shapes=[pltpu.SemaphoreType.DMA],
        compiler_params=_params(1, 2 * tm * d * 4),
        name="scatter_rows",
    )(pos3, hn)


def _expert_body(vblk, vexp, vfirst, vnew, vvalid, estart,
                 x_ref, w1_ref, w3_ref, w2_ref, o_ref, w1b, w3b, w2b):
    bm = x_ref.shape[0]
    v = pl.program_id(0)

    @pl.when(vvalid[v] == 1)
    def _():
        @pl.when(vnew[v] == 1)
        def _():
            w1b[...] = w1_ref[0, 0].astype(BF16)
            w3b[...] = w3_ref[0, 0].astype(BF16)
            w2b[...] = w2_ref[0, 0].astype(BF16)

        xb = x_ref[...].astype(BF16)
        h1 = jnp.dot(xb, w1b[...], preferred_element_type=F32)
        h3 = jnp.dot(xb, w3b[...], preferred_element_type=F32)
        hid = (h1 * _sigmoid(h1)) * h3
        y = jnp.dot(hid.astype(BF16), w2b[...], preferred_element_type=F32)

        @pl.when(vfirst[v] == 1)
        def _():
            o_ref[...] = y

        @pl.when(vfirst[v] == 0)
        def _():
            e = vexp[v]
            r = vblk[v] * bm + lax.broadcasted_iota(I32, (bm, 1), 0)
            mine = (r >= estart[e]) & (r < estart[e + 1])
            o_ref[...] = jnp.where(mine, y, o_ref[...])


def _experts(xs, w1, w3, w2, tables, *, layer, bm):
    n, d = xs.shape
    f = w1.shape[3]
    n_visits = tables[0].shape[0]
    nbytes = 2 * (bm * d * 4 * 2 + 3 * d * f * 4) + 3 * d * f * 2 + 6 * bm * d * 4
    wmap = lambda v, blk, exp, *_: (layer, exp[v], 0, 0)
    grid_spec = pltpu.PrefetchScalarGridSpec(
        num_scalar_prefetch=len(tables),
        grid=(n_visits,),
        in_specs=[pl.BlockSpec((bm, d), lambda v, blk, exp, *_: (blk[v], 0)),
                  pl.BlockSpec((1, 1, d, f), wmap),
                  pl.BlockSpec((1, 1, d, f), wmap),
                  pl.BlockSpec((1, 1, f, d), wmap)],
        out_specs=pl.BlockSpec((bm, d), lambda v, blk, exp, *_: (blk[v], 0)),
        scratch_shapes=[pltpu.VMEM((d, f), BF16), pltpu.VMEM((d, f), BF16),
                        pltpu.VMEM((f, d), BF16)])
    return pl.pallas_call(
        _expert_body,
        grid_spec=grid_spec,
        out_shape=jax.ShapeDtypeStruct((n, d), F32),
        compiler_params=_params(1, nbytes),
        name="experts",
    )(*tables, xs, w1, w3, w2)


def _visit_tables(counts, n_slots, bm):
    n_blocks = n_slots // bm
    n_visits = n_blocks + N_EXPERTS - 1
    end = jnp.cumsum(counts)
    start = end - counts
    first_blk = start // bm
    last_blk = jnp.maximum(end - 1, 0) // bm
    per_e = jnp.where(counts > 0, last_blk - first_blk + 1, 0)
    cum = jnp.cumsum(per_e)
    total = cum[-1]
    v = jnp.minimum(jnp.arange(n_visits, dtype=I32), total - 1)
    e = jnp.sum((cum[None, :] <= v[:, None]).astype(I32), axis=1)
    blk = (first_blk[e] + (v - (cum[e] - per_e[e]))).astype(I32)
    valid = (jnp.arange(n_visits) < total).astype(I32)
    prev_blk = jnp.concatenate([jnp.full((1,), -1, I32), blk[:-1]])
    prev_e = jnp.concatenate([jnp.full((1,), -1, I32), e[:-1]])
    first = (blk != prev_blk).astype(I32)
    new = (e != prev_e).astype(I32)
    estart = jnp.concatenate([start, end[-1:]]).astype(I32)
    return blk, e, first, new, valid, estart


def _combine_body(pos_ref, x_ref, gt_ref, g_ref, ys_ref, o_ref, ybuf, sem, *, final_norm):
    tm = x_ref.shape[0]

    def issue(r, carry):
        for k in range(TOP_K):
            p = pos_ref[0, 0, k * tm + r]
            pltpu.make_async_copy(ys_ref.at[pl.ds(p, 1), :], ybuf.at[k, pl.ds(r, 1), :], sem).start()
        return carry

    lax.fori_loop(0, tm, issue, 0, unroll=ISSUE_UNROLL)
    for k in range(TOP_K):
        _wait_rows(ys_ref.at[pl.ds(0, tm), :], ybuf.at[k], sem)
    x2 = x_ref[...] + gt_ref[:, 0:1] * ybuf[0] + gt_ref[:, 1:2] * ybuf[1]
    if final_norm:
        x2 = _rms_norm(x2, g_ref[...])
    o_ref[...] = x2


def _combine(x1, ys, pos3, gates_t, g_final, *, tm, final_norm):
    t, d = x1.shape
    nbytes = 2 * (2 * tm * d * 4 + tm * LANES * 4) + TOP_K * tm * d * 4 + 2 * tm * d * 4
    return pl.pallas_call(
        functools.partial(_combine_body, final_norm=final_norm),
        grid=(t // tm,),
        in_specs=[_pos_spec(tm),
                  pl.BlockSpec((tm, d), lambda i: (i, 0)),
                  pl.BlockSpec((tm, TOP_K), lambda i: (i, 0)),
                  pl.BlockSpec((1, d), lambda i: (0, 0)),
                  pl.BlockSpec(memory_space=pl.ANY)],
        out_specs=pl.BlockSpec((tm, d), lambda i: (i, 0)),
        out_shape=jax.ShapeDtypeStruct((t, d), F32),
        scratch_shapes=[pltpu.VMEM((TOP_K, tm, d), F32), pltpu.SemaphoreType.DMA],
        compiler_params=_params(1, nbytes),
        name="combine",
    )(pos3, x1, gates_t, g_final.reshape(1, d), ys)


def _tiles(t):
    return dict(tm_front=min(256, t), tr=min(2048, t), tm_rows=min(256, t), bm=min(256, TOP_K * t))


def _router_params(w_group, b_group, w_router, b_router):
    d = w_group.shape[0]
    gpad = SUBLANES - N_GROUPS
    wr = jnp.concatenate([w_group, jnp.zeros((d, gpad), F32), w_router,
                          jnp.zeros((d, LANES - ROUTER_ROWS), F32)], axis=1)
    br = jnp.concatenate([b_group, jnp.zeros((gpad,), F32), b_router,
                          jnp.zeros((LANES - ROUTER_ROWS,), F32)]).reshape(1, LANES)
    return wr, br


def _moe(x1, hn, logits, w1, w3, w2, layer, g_final, final_norm, tl):
    t, d = x1.shape
    meta, gates, cnt = _route(logits, tr=tl["tr"])

    counts = cnt[:, 0].astype(I32)
    start = jnp.cumsum(counts) - counts
    is_e = meta[0:TOP_K, :, None] == jnp.arange(N_EXPERTS, dtype=I32)
    pos = jnp.sum(jnp.where(is_e, start, 0), axis=-1) + meta[TOP_K:2 * TOP_K]
    tm = tl["tm_rows"]
    pos3 = pos.reshape(TOP_K, t // tm, tm).transpose(1, 0, 2).reshape(t // tm, 1, TOP_K * tm)

    xs = _scatter_rows(hn, pos3, tm=tm)
    tables = _visit_tables(counts, TOP_K * t, tl["bm"])
    ys = _experts(xs, w1, w3, w2, tables, layer=layer, bm=tl["bm"])
    return _combine(x1, ys, pos3, gates[0:TOP_K].T, g_final, tm=tm, final_norm=final_norm)


def kernel(x, ln_mix, ln_ffn, ln_final, ev_w_in, ev_conv_w, ev_sgu_ln_g, ev_sgu_ln_b, ev_sgu_w, ev_sgu_b, ev_w_out, od_w_in, od_pool_w, od_pool_scale, od_dw_w, od_dw_b, od_ln_g, od_ln_b, od_w_out, moe_w_group, moe_b_group, moe_w_router, moe_b_router, moe_w1, moe_w3, moe_w2):
    bsz, s, d = x.shape
    assert bsz == 1, "token mixers carry state across row tiles of one sequence"
    depth = ln_mix.shape[0]
    tl = _tiles(s)
    tm = tl["tm_front"]
    xf = x.reshape(s, d)
    for layer in range(depth):
        i = layer // 2
        wr, br = _router_params(moe_w_group[layer], moe_b_group[layer],
                                moe_w_router[layer], moe_b_router[layer])
        if layer % 2 == 0:
            c = ev_conv_w.shape[2]
            mix_params = [ev_conv_w[i], ev_sgu_ln_g[i].reshape(1, c), ev_sgu_ln_b[i].reshape(1, c),
                          ev_sgu_w[i], ev_sgu_b[i].T]
            mix_scratch = [pltpu.VMEM((tm + EVEN_HALO, c), F32)]
            x1, hn, logits = _layer_front(
                _even_layer_body, xf, ln_mix[layer], ev_w_in[i].astype(BF16), mix_params, mix_scratch,
                ev_w_out[i].astype(BF16), ln_ffn[layer], wr, br, tm=tm)
        else:
            c = od_dw_w.shape[2]
            mix_params = [od_pool_w[i], od_pool_scale[i].reshape(1, c), od_dw_w[i],
                          od_dw_b[i].reshape(1, c), od_ln_g[i].reshape(1, c), od_ln_b[i].reshape(1, c)]
            mix_scratch = [pltpu.VMEM((tm + POOL_HALO, c), F32), pltpu.VMEM((tm + CONF_HALO, c), F32),
                           pltpu.VMEM((tm, c), F32)]
            x1, hn, logits = _layer_front(
                _odd_layer_body, xf, ln_mix[layer], od_w_in[i].astype(BF16), mix_params, mix_scratch,
                od_w_out[i].astype(BF16), ln_ffn[layer], wr, br, tm=tm)
        xf = _moe(x1, hn, logits, moe_w1, moe_w3, moe_w2, layer, ln_final, layer == depth - 1, tl)
    return xf.reshape(bsz, s, d)
```

```python
import functools
import math

import jax
import jax.numpy as jnp
from jax import lax
from jax.experimental import pallas as pl
from jax.experimental.pallas import tpu as pltpu

F32 = jnp.float32
BF16 = jnp.bfloat16
I32 = jnp.int32

EPS = 1e-6
CHUNK = 64
SGU_BLOCK = 128
SGU_HEADS = 8
POOL_WINDOWS = (2, 4, 8, 16)
N_GROUPS = 4
EXPERTS_PER_GROUP = 8
N_EXPERTS = N_GROUPS * EXPERTS_PER_GROUP
TOP_K = 2

LANES = 128
SUBLANES = 8
VMEM_CAP_BYTES = 60 * 1024 * 1024

ROUTER_ROWS = SUBLANES + N_EXPERTS
NEG_BIG = -1e30


def _vmem_limit(nbytes):
    return int(min(VMEM_CAP_BYTES, nbytes * 5 // 4 + (6 << 20)))


def _params(n_axes, nbytes):
    return pltpu.CompilerParams(
        dimension_semantics=("arbitrary",) * n_axes,
        vmem_limit_bytes=_vmem_limit(nbytes))


def _const_spec(shape, single_buffer=False):
    mode = dict(pipeline_mode=pl.Buffered(1)) if single_buffer else {}
    return pl.BlockSpec(shape, lambda i: (0,) * len(shape), **mode)


def _sigmoid(x):
    return 1.0 / (1.0 + jnp.exp(-x))


def _gelu_tanh(x):
    c = math.sqrt(2.0 / math.pi)
    return x * (0.5 * (1.0 + jnp.tanh(c * (x + 0.044715 * (x * x * x)))))


def _layer_norm(v, g, b):
    mu = jnp.mean(v, axis=-1, keepdims=True)
    vc = v - mu
    var = jnp.mean(vc * vc, axis=-1, keepdims=True)
    return vc * lax.rsqrt(var + EPS) * g + b


def _rms_norm(x, g):
    ms = jnp.mean(x * x, axis=-1, keepdims=True)
    return x * lax.rsqrt(ms + EPS) * g


def _split_bf16(v):
    head = v.astype(BF16)
    return head, (v - head.astype(F32)).astype(BF16)


IN_PROJ_COLS = 1024


def _in_proj(x_ref, g_ref, w_ref, z_ref):
    h = _rms_norm(x_ref[...], g_ref[...]).astype(BF16)
    for c0 in range(0, w_ref.shape[1], IN_PROJ_COLS):
        cols = slice(c0, c0 + IN_PROJ_COLS)
        z_ref[:, cols] = jnp.dot(h, w_ref[:, cols], preferred_element_type=F32)


def _out_proj_route(x_ref, y_ref, w_ref, g_ref, wh_ref, wl_ref, br_ref, x1_ref, hn_ref, lg_ref):
    x1 = x_ref[...] + jnp.dot(y_ref[...], w_ref[...], preferred_element_type=F32)
    x1_ref[...] = x1
    hn = _rms_norm(x1, g_ref[...])
    hn_ref[...] = hn
    head, rem = _split_bf16(hn)
    lg_ref[...] = (jnp.dot(head, wh_ref[...], preferred_element_type=F32)
                   + jnp.dot(rem, wh_ref[...], preferred_element_type=F32)
                   + jnp.dot(head, wl_ref[...], preferred_element_type=F32)
                   + br_ref[...])


EVEN_HALO = SUBLANES


def _even_mix(z_ref, cw_ref, lg_ref, lb_ref, sw_ref, sb_ref, o_ref, pbuf):
    tm = z_ref.shape[0]
    kw, c = cw_ref.shape
    col = lambda j: z_ref[:, j * c:(j + 1) * c]

    @pl.when(pl.program_id(0) == 0)
    def _():
        pbuf[0:EVEN_HALO, :] = jnp.zeros((EVEN_HALO, c), F32)

    p = col(1) * col(2)
    pbuf[EVEN_HALO:EVEN_HALO + tm, :] = p
    conv = cw_ref[kw - 1:kw, :] * p
    for k in range(kw - 1):
        conv = conv + cw_ref[k:k + 1, :] * pbuf[pl.ds(EVEN_HALO - (kw - 1) + k, tm), :]
    o_ref[:, 0:c] = (col(0) * conv).astype(BF16)
    pbuf[0:EVEN_HALO, :] = pbuf[tm:tm + EVEN_HALO, :]

    u = _gelu_tanh(col(3))
    vn = _layer_norm(_gelu_tanh(col(4)), lg_ref[...], lb_ref[...]).astype(BF16)
    tpos = lax.broadcasted_iota(I32, (SGU_BLOCK, SGU_BLOCK), 0)
    spos = lax.broadcasted_iota(I32, (SGU_BLOCK, SGU_BLOCK), 1)
    mask = (spos // CHUNK) <= (tpos // CHUNK)
    hd = c // SGU_HEADS
    for h in range(SGU_HEADS):
        wh = jnp.where(mask, sw_ref[h], 0.0).astype(BF16)
        bh = sb_ref[:, h:h + 1]
        cols = slice(h * hd, (h + 1) * hd)
        for n in range(tm // SGU_BLOCK):
            rows = slice(n * SGU_BLOCK, (n + 1) * SGU_BLOCK)
            sv = jnp.dot(wh, vn[rows, cols], preferred_element_type=F32) + bh
            o_ref[rows, c + h * hd:c + (h + 1) * hd] = (u[rows, cols] * sv).astype(BF16)


POOL_HALO = 24
CONF_HALO = 32
CONV_ROWS = 128


def _window_sum(xbuf, cols, w, lv_a, lv_b, tm):
    lo, hi = SUBLANES, POOL_HALO + tm
    src, src_cols, n = xbuf, cols, 1
    dst, other = lv_a, lv_b
    while 2 * n < w:
        dst[lo:hi, :] = src[lo:hi, src_cols] + src[pl.ds(lo - n, hi - lo), src_cols]
        src, src_cols, n = dst, slice(None), 2 * n
        dst, other = other, dst
    return src[POOL_HALO:hi, src_cols] + src[pl.ds(POOL_HALO - n, tm), src_cols]


def _odd_mix(z_ref, pw_ref, ps_ref, dw_ref, db_ref, lg_ref, lb_ref, o_ref,
             xbuf, lv_a, lv_b, gbuf, sbuf, dbuf):
    tm = z_ref.shape[0]
    kw, c = dw_ref.shape
    i = pl.program_id(0)

    @pl.when(i == 0)
    def _():
        xbuf[0:POOL_HALO, :] = jnp.zeros((POOL_HALO, c), F32)
        gbuf[0:CONF_HALO, :] = jnp.zeros((CONF_HALO, c), F32)
        lv_a[0:SUBLANES, :] = jnp.zeros((SUBLANES, lv_a.shape[1]), F32)
        lv_b[0:SUBLANES, :] = jnp.zeros((SUBLANES, lv_b.shape[1]), F32)

    x = z_ref[:, 0:c]
    xbuf[POOL_HALO:POOL_HALO + tm, :] = x
    step = i * tm + lax.broadcasted_iota(I32, (tm, 1), 0) + 1
    gd = c // len(POOL_WINDOWS)
    for g, w in enumerate(POOL_WINDOWS):
        cols = slice(g * gd, (g + 1) * gd)
        s = _window_sum(xbuf, cols, w, lv_a, lv_b, tm)
        inv_count = 1.0 / jnp.minimum(step, w).astype(F32)
        d = s * inv_count - x[:, cols]
        y = jnp.dot(d.astype(BF16), pw_ref[g].astype(BF16), preferred_element_type=F32)
        o_ref[:, cols] = (y * ps_ref[:, cols]).astype(BF16)
    xbuf[0:POOL_HALO, :] = xbuf[tm:tm + POOL_HALO, :]

    gbuf[CONF_HALO:CONF_HALO + tm, :] = z_ref[:, c:2 * c] * _sigmoid(z_ref[:, 2 * c:3 * c])
    n_rows = CONF_HALO + tm

    def conv_cols(cb, carry):
        lanes = pl.ds(pl.multiple_of(cb * LANES, LANES), LANES)
        for b in range(1, SUBLANES):
            sbuf[b - 1, SUBLANES:n_rows, :] = gbuf[pl.ds(SUBLANES - b, n_rows - SUBLANES), lanes]
        accs = [jnp.broadcast_to(db_ref[:, lanes], (CONV_ROWS, LANES)) for _ in range(0, tm, CONV_ROWS)]
        for back in range(kw):
            a, b = divmod(back, SUBLANES)
            wk = dw_ref[kw - 1 - back:kw - back, lanes]
            for n, r0 in enumerate(range(0, tm, CONV_ROWS)):
                start = r0 + CONF_HALO - SUBLANES * a
                tap = gbuf[start:start + CONV_ROWS, lanes] if b == 0 else sbuf[b - 1, start:start + CONV_ROWS, :]
                accs[n] = accs[n] + wk * tap
        for n, r0 in enumerate(range(0, tm, CONV_ROWS)):
            dbuf[r0:r0 + CONV_ROWS, lanes] = accs[n]
        return carry

    lax.fori_loop(0, c // LANES, conv_cols, 0)
    gbuf[0:CONF_HALO, :] = gbuf[tm:tm + CONF_HALO, :]
    yn = _layer_norm(dbuf[...], lg_ref[...], lb_ref[...])
    o_ref[:, c:2 * c] = (yn * _sigmoid(yn)).astype(BF16)


def _even_layer_body(x_ref, gm_ref, win_ref, cw_ref, lg_ref, lb_ref, sw_ref, sb_ref,
                     wout_ref, gf_ref, wh_ref, wl_ref, br_ref,
                     x1_ref, hn_ref, lgt_ref, z_ref, y_ref, pbuf):
    _in_proj(x_ref, gm_ref, win_ref, z_ref)
    _even_mix(z_ref, cw_ref, lg_ref, lb_ref, sw_ref, sb_ref, y_ref, pbuf)
    _out_proj_route(x_ref, y_ref, wout_ref, gf_ref, wh_ref, wl_ref, br_ref, x1_ref, hn_ref, lgt_ref)


def _odd_layer_body(x_ref, gm_ref, win_ref, pw_ref, ps_ref, dw_ref, db_ref, lg_ref, lb_ref,
                    wout_ref, gf_ref, wh_ref, wl_ref, br_ref,
                    x1_ref, hn_ref, lgt_ref, z_ref, y_ref, *mix_scratch):
    _in_proj(x_ref, gm_ref, win_ref, z_ref)
    _odd_mix(z_ref, pw_ref, ps_ref, dw_ref, db_ref, lg_ref, lb_ref, y_ref, *mix_scratch)
    _out_proj_route(x_ref, y_ref, wout_ref, gf_ref, wh_ref, wl_ref, br_ref, x1_ref, hn_ref, lgt_ref)


def _layer_front(body, x, g_mix, w_in, mix_params, mix_scratch, w_out, g_ffn, wr, br, *, tm):
    t, d = x.shape
    n = w_in.shape[1]
    c2 = w_out.shape[0]
    wh, wl = _split_bf16(wr)
    row = lambda width: pl.BlockSpec((tm, width), lambda i: (i, 0))
    small = [g_mix.reshape(1, d)] , mix_params, [g_ffn.reshape(1, d), wh, wl, br]
    scratch = [pltpu.VMEM((tm, n), F32), pltpu.VMEM((tm, c2), BF16)] + mix_scratch
    nbytes = (d * n * 2 + c2 * d * 2
              + 2 * 3 * tm * d * 4 + 2 * tm * LANES * 4
              + tm * n * 4 + tm * c2 * 2
              + sum(math.prod(s.shape) * 4 for s in mix_scratch)
              + 2 * sum(p.size * p.dtype.itemsize for p in mix_params + small[2][1:])
              + 8 * tm * d * 4)
    return pl.pallas_call(
        body,
        grid=(t // tm,),
        in_specs=([row(d), _const_spec((1, d)), _const_spec(w_in.shape, single_buffer=True)]
                  + [_const_spec(p.shape) for p in mix_params]
                  + [_const_spec(w_out.shape, single_buffer=True), _const_spec((1, d)),
                     _const_spec(wh.shape), _const_spec(wl.shape), _const_spec(br.shape)]),
        out_specs=[row(d), row(d), row(LANES)],
        out_shape=[jax.ShapeDtypeStruct((t, d), F32),
                   jax.ShapeDtypeStruct((t, d), F32),
                   jax.ShapeDtypeStruct((t, LANES), F32)],
        scratch_shapes=scratch,
        compiler_params=_params(1, nbytes),
        name=body.__name__.strip("_").replace("_body", ""),
    )(x, small[0][0], w_in, *mix_params, w_out, *small[2])


RANK_BLOCK = 256


def _route_block(logits, run, before):
    n = logits.shape[1]
    row8 = lax.broadcasted_iota(I32, (SUBLANES, n), 0)
    row = lax.broadcasted_iota(I32, (N_EXPERTS, n), 0)

    gl = jnp.where(row8 < N_GROUPS, logits[0:SUBLANES], NEG_BIG)
    gmax = jnp.max(gl, axis=0, keepdims=True)
    gidx = jnp.min(jnp.where(gl == gmax, row8, N_EXPERTS), axis=0, keepdims=True)
    p_group = 1.0 / jnp.sum(jnp.exp(gl - gmax), axis=0, keepdims=True)

    el = jnp.where(row // EXPERTS_PER_GROUP == gidx, logits[SUBLANES:ROUTER_ROWS], NEG_BIG)
    m0 = jnp.max(el, axis=0, keepdims=True)
    e0 = jnp.min(jnp.where(el == m0, row, N_EXPERTS), axis=0, keepdims=True)
    el1 = jnp.where(row == e0, NEG_BIG, el)
    m1 = jnp.max(el1, axis=0, keepdims=True)
    e1 = jnp.min(jnp.where(el1 == m1, row, N_EXPERTS), axis=0, keepdims=True)
    ratio = jnp.exp(m1 - m0)
    gate0 = p_group / (1.0 + ratio)
    gate1 = p_group * ratio / (1.0 + ratio)

    sel0 = row == e0
    sel1 = row == e1
    pre0 = jnp.dot(sel0.astype(BF16), before, preferred_element_type=F32)
    pre1 = jnp.dot(sel1.astype(BF16), before, preferred_element_type=F32)
    cnt0 = jnp.sum(sel0.astype(F32), axis=1, keepdims=True)
    cnt1 = jnp.sum(sel1.astype(F32), axis=1, keepdims=True)
    rank0 = jnp.sum(jnp.where(sel0, pre0 + run, 0.0), axis=0, keepdims=True)
    rank1 = jnp.sum(jnp.where(sel1, pre1 + (run + cnt0), 0.0), axis=0, keepdims=True)

    meta = jnp.where(row8 == 0, e0, jnp.where(row8 == 1, e1, jnp.where(
        row8 == 2, rank0.astype(I32), jnp.where(row8 == 3, rank1.astype(I32), 0))))
    gates = jnp.where(row8 == 0, gate0, jnp.where(row8 == 1, gate1, 0.0))
    return meta, gates, run + (cnt0 + cnt1)


def _route_body(lg_ref, meta_ref, gate_ref, cnt_ref, run_ref):
    tr = lg_ref.shape[0]
    nb = min(RANK_BLOCK, tr)

    @pl.when(pl.program_id(0) == 0)
    def _():
        run_ref[...] = jnp.zeros(run_ref.shape, F32)

    src = lax.broadcasted_iota(I32, (nb, nb), 0)
    dst = lax.broadcasted_iota(I32, (nb, nb), 1)
    before = (src < dst).astype(BF16)
    run = run_ref[:, 0:1]
    for b in range(tr // nb):
        cols = slice(b * nb, (b + 1) * nb)
        logits_t = lg_ref[cols, :].T[0:ROUTER_ROWS]
        meta, gates, run = _route_block(logits_t, run, before)
        meta_ref[:, cols] = meta
        gate_ref[:, cols] = gates
    run_ref[...] = jnp.broadcast_to(run, run_ref.shape)
    cnt_ref[...] = run_ref[...]


def _route(logits, *, tr):
    t = logits.shape[0]
    return pl.pallas_call(
        _route_body,
        grid=(t // tr,),
        in_specs=[pl.BlockSpec((tr, LANES), lambda i: (i, 0))],
        out_specs=[pl.BlockSpec((SUBLANES, tr), lambda i: (0, i)),
                   pl.BlockSpec((SUBLANES, tr), lambda i: (0, i)),
                   pl.BlockSpec((N_EXPERTS, LANES), lambda i: (0, 0))],
        out_shape=[jax.ShapeDtypeStruct((SUBLANES, t), I32),
                   jax.ShapeDtypeStruct((SUBLANES, t), F32),
                   jax.ShapeDtypeStruct((N_EXPERTS, LANES), F32)],
        scratch_shapes=[pltpu.VMEM((N_EXPERTS, LANES), F32)],
        compiler_params=_params(1, 64 * tr * 4 * 4),
        name="route",
    )(logits)


ISSUE_UNROLL = 8


def _pos_spec(tm):
    return pl.BlockSpec((1, 1, TOP_K * tm), lambda i: (i, 0, 0), memory_space=pltpu.SMEM)


def _wait_rows(src_ref, dst_ref, sem):
    pltpu.make_async_copy(src_ref, dst_ref, sem).wait()


def _scatter_body(pos_ref, h_ref, xs_ref, sem):
    tm = h_ref.shape[0]

    for r in range(tm):
        for k in range(TOP_K):
            p = pos_ref[0, 0, k * tm + r]
            pltpu.make_async_copy(h_ref.at[pl.ds(r, 1), :], xs_ref.at[pl.ds(p, 1), :], sem).start()
    for k in range(TOP_K):
        _wait_rows(h_ref, xs_ref.at[pl.ds(0, tm), :], sem)


def _scatter_rows(hn, pos3, *, tm):
    t, d = hn.shape
    return pl.pallas_call(
        _scatter_body,
        grid=(t // tm,),
        in_specs=[_pos_spec(tm), pl.BlockSpec((tm, d), lambda i: (i, 0))],
        out_specs=pl.BlockSpec(memory_space=pl.ANY),
        out_shape=jax.ShapeDtypeStruct((TOP_K * t, d), hn.dtype),
        scratch_shapes=[pltpu.SemaphoreType.DMA],
        compiler_params=_params(1, 2 * tm * d * 4),
        name="scatter_rows",
    )(pos3, hn)


def _expert_body(vblk, vexp, vfirst, vnew, vvalid, estart,
                 x_ref, w1_ref, w3_ref, w2_ref, o_ref, w1b, w3b, w2b):
    bm = x_ref.shape[0]
    v = pl.program_id(0)

    @pl.when(vvalid[v] == 1)
    def _():
        @pl.when(vnew[v] == 1)
        def _():
            w1b[...] = w1_ref[0, 0].astype(BF16)
            w3b[...] = w3_ref[0, 0].astype(BF16)
            w2b[...] = w2_ref[0, 0].astype(BF16)

        xb = x_ref[...].astype(BF16)
        h1 = jnp.dot(xb, w1b[...], preferred_element_type=F32)
        h3 = jnp.dot(xb, w3b[...], preferred_element_type=F32)
        hid = (h1 * _sigmoid(h1)) * h3
        y = jnp.dot(hid.astype(BF16), w2b[...], preferred_element_type=F32)

        @pl.when(vfirst[v] == 1)
        def _():
            o_ref[...] = y

        @pl.when(vfirst[v] == 0)
        def _():
            e = vexp[v]
            r = vblk[v] * bm + lax.broadcasted_iota(I32, (bm, 1), 0)
            mine = (r >= estart[e]) & (r < estart[e + 1])
            o_ref[...] = jnp.where(mine, y, o_ref[...])


def _experts(xs, w1, w3, w2, tables, *, layer, bm):
    n, d = xs.shape
    f = w1.shape[3]
    n_visits = tables[0].shape[0]
    nbytes = 2 * (bm * d * 4 * 2 + 3 * d * f * 4) + 3 * d * f * 2 + 6 * bm * d * 4
    wmap = lambda v, blk, exp, *_: (layer, exp[v], 0, 0)
    grid_spec = pltpu.PrefetchScalarGridSpec(
        num_scalar_prefetch=len(tables),
        grid=(n_visits,),
        in_specs=[pl.BlockSpec((bm, d), lambda v, blk, exp, *_: (blk[v], 0)),
                  pl.BlockSpec((1, 1, d, f), wmap),
                  pl.BlockSpec((1, 1, d, f), wmap),
                  pl.BlockSpec((1, 1, f, d), wmap)],
        out_specs=pl.BlockSpec((bm, d), lambda v, blk, exp, *_: (blk[v], 0)),
        scratch_shapes=[pltpu.VMEM((d, f), BF16), pltpu.VMEM((d, f), BF16),
                        pltpu.VMEM((f, d), BF16)])
    return pl.pallas_call(
        _expert_body,
        grid_spec=grid_spec,
        out_shape=jax.ShapeDtypeStruct((n, d), F32),
        compiler_params=_params(1, nbytes),
        name="experts",
    )(*tables, xs, w1, w3, w2)


def _visit_tables(counts, n_slots, bm):
    n_blocks = n_slots // bm
    n_visits = n_blocks + N_EXPERTS - 1
    end = jnp.cumsum(counts)
    start = end - counts
    first_blk = start // bm
    last_blk = jnp.maximum(end - 1, 0) // bm
    per_e = jnp.where(counts > 0, last_blk - first_blk + 1, 0)
    cum = jnp.cumsum(per_e)
    total = cum[-1]
    v = jnp.minimum(jnp.arange(n_visits, dtype=I32), total - 1)
    e = jnp.sum((cum[None, :] <= v[:, None]).astype(I32), axis=1)
    blk = (first_blk[e] + (v - (cum[e] - per_e[e]))).astype(I32)
    valid = (jnp.arange(n_visits) < total).astype(I32)
    prev_blk = jnp.concatenate([jnp.full((1,), -1, I32), blk[:-1]])
    prev_e = jnp.concatenate([jnp.full((1,), -1, I32), e[:-1]])
    first = (blk != prev_blk).astype(I32)
    new = (e != prev_e).astype(I32)
    estart = jnp.concatenate([start, end[-1:]]).astype(I32)
    return blk, e, first, new, valid, estart


def _combine_body(pos_ref, x_ref, gt_ref, g_ref, ys_ref, o_ref, ybuf, sem, *, final_norm):
    tm = x_ref.shape[0]

    for r in range(tm):
        for k in range(TOP_K):
            p = pos_ref[0, 0, k * tm + r]
            pltpu.make_async_copy(ys_ref.at[pl.ds(p, 1), :], ybuf.at[k, pl.ds(r, 1), :], sem).start()
    for k in range(TOP_K):
        _wait_rows(ys_ref.at[pl.ds(0, tm), :], ybuf.at[k], sem)
    x2 = x_ref[...] + gt_ref[:, 0:1] * ybuf[0] + gt_ref[:, 1:2] * ybuf[1]
    if final_norm:
        x2 = _rms_norm(x2, g_ref[...])
    o_ref[...] = x2


def _combine(x1, ys, pos3, gates_t, g_final, *, tm, final_norm):
    t, d = x1.shape
    nbytes = 2 * (2 * tm * d * 4 + tm * LANES * 4) + TOP_K * tm * d * 4 + 2 * tm * d * 4
    return pl.pallas_call(
        functools.partial(_combine_body, final_norm=final_norm),
        grid=(t // tm,),
        in_specs=[_pos_spec(tm),
                  pl.BlockSpec((tm, d), lambda i: (i, 0)),
                  pl.BlockSpec((tm, TOP_K), lambda i: (i, 0)),
                  pl.BlockSpec((1, d), lambda i: (0, 0)),
                  pl.BlockSpec(memory_space=pl.ANY)],
        out_specs=pl.BlockSpec((tm, d), lambda i: (i, 0)),
        out_shape=jax.ShapeDtypeStruct((t, d), F32),
        scratch_shapes=[pltpu.VMEM((TOP_K, tm, d), F32), pltpu.SemaphoreType.DMA],
        compiler_params=_params(1, nbytes),
        name="combine",
    )(pos3, x1, gates_t, g_final.reshape(1, d), ys)


def _tiles(t):
    return dict(tm_front=min(256, t), tr=min(2048, t), tm_rows=min(256, t), bm=min(256, TOP_K * t))


def _router_params(w_group, b_group, w_router, b_router):
    d = w_group.shape[0]
    gpad = SUBLANES - N_GROUPS
    wr = jnp.concatenate([w_group, jnp.zeros((d, gpad), F32), w_router,
                          jnp.zeros((d, LANES - ROUTER_ROWS), F32)], axis=1)
    br = jnp.concatenate([b_group, jnp.zeros((gpad,), F32), b_router,
                          jnp.zeros((LANES - ROUTER_ROWS,), F32)]).reshape(1, LANES)
    return wr, br


def _moe(x1, hn, logits, w1, w3, w2, layer, g_final, final_norm, tl):
    t, d = x1.shape
    meta, gates, cnt = _route(logits, tr=tl["tr"])

    counts = cnt[:, 0].astype(I32)
    start = jnp.cumsum(counts) - counts
    is_e = meta[0:TOP_K, :, None] == jnp.arange(N_EXPERTS, dtype=I32)
    pos = jnp.sum(jnp.where(is_e, start, 0), axis=-1) + meta[TOP_K:2 * TOP_K]
    tm = tl["tm_rows"]
    pos3 = pos.reshape(TOP_K, t // tm, tm).transpose(1, 0, 2).reshape(t // tm, 1, TOP_K * tm)

    xs = _scatter_rows(hn, pos3, tm=tm)
    tables = _visit_tables(counts, TOP_K * t, tl["bm"])
    ys = _experts(xs, w1, w3, w2, tables, layer=layer, bm=tl["bm"])
    return _combine(x1, ys, pos3, gates[0:TOP_K].T, g_final, tm=tm, final_norm=final_norm)


def kernel(x, ln_mix, ln_ffn, ln_final, ev_w_in, ev_conv_w, ev_sgu_ln_g, ev_sgu_ln_b, ev_sgu_w, ev_sgu_b, ev_w_out, od_w_in, od_pool_w, od_pool_scale, od_dw_w, od_dw_b, od_ln_g, od_ln_b, od_w_out, moe_w_group, moe_b_group, moe_w_router, moe_b_router, moe_w1, moe_w3, moe_w2):
    bsz, s, d = x.shape
    assert bsz == 1, "token mixers carry state across row tiles of one sequence"
    depth = ln_mix.shape[0]
    tl = _tiles(s)
    tm = tl["tm_front"]
    xf = x.reshape(s, d)
    for layer in range(depth):
        i = layer // 2
        wr, br = _router_params(moe_w_group[layer], moe_b_group[layer],
                                moe_w_router[layer], moe_b_router[layer])
        if layer % 2 == 0:
            c = ev_conv_w.shape[2]
            mix_params = [ev_conv_w[i], ev_sgu_ln_g[i].reshape(1, c), ev_sgu_ln_b[i].reshape(1, c),
                          ev_sgu_w[i], ev_sgu_b[i].T]
            mix_scratch = [pltpu.VMEM((tm + EVEN_HALO, c), F32)]
            x1, hn, logits = _layer_front(
                _even_layer_body, xf, ln_mix[layer], ev_w_in[i].astype(BF16), mix_params, mix_scratch,
                ev_w_out[i].astype(BF16), ln_ffn[layer], wr, br, tm=tm)
        else:
            c = od_dw_w.shape[2]
            mix_params = [od_pool_w[i], od_pool_scale[i].reshape(1, c), od_dw_w[i],
                          od_dw_b[i].reshape(1, c), od_ln_g[i].reshape(1, c), od_ln_b[i].reshape(1, c)]
            gd = c // len(POOL_WINDOWS)
            mix_scratch = [pltpu.VMEM((tm + POOL_HALO, c), F32),
                           pltpu.VMEM((tm + POOL_HALO, gd), F32), pltpu.VMEM((tm + POOL_HALO, gd), F32),
                           pltpu.VMEM((tm + CONF_HALO, c), F32),
                           pltpu.VMEM((SUBLANES - 1, tm + CONF_HALO, LANES), F32),
                           pltpu.VMEM((tm, c), F32)]
            x1, hn, logits = _layer_front(
                _odd_layer_body, xf, ln_mix[layer], od_w_in[i].astype(BF16), mix_params, mix_scratch,
                od_w_out[i].astype(BF16), ln_ffn[layer], wr, br, tm=tm)
        xf = _moe(x1, hn, logits, moe_w1, moe_w3, moe_w2, layer, ln_final, layer == depth - 1, tl)
    return xf.reshape(bsz, s, d)
```

```python
import functools
import math

import jax
import jax.numpy as jnp
from jax import lax
from jax.experimental import pallas as pl
from jax.experimental.pallas import tpu as pltpu

F32 = jnp.float32
BF16 = jnp.bfloat16
I32 = jnp.int32

EPS = 1e-6
CHUNK = 64
SGU_BLOCK = 128
SGU_HEADS = 8
POOL_WINDOWS = (2, 4, 8, 16)
N_GROUPS = 4
EXPERTS_PER_GROUP = 8
N_EXPERTS = N_GROUPS * EXPERTS_PER_GROUP
TOP_K = 2

LANES = 128
SUBLANES = 8
VMEM_CAP_BYTES = 60 * 1024 * 1024

ROUTER_ROWS = SUBLANES + N_EXPERTS
NEG_BIG = -1e30


def _vmem_limit(nbytes):
    return int(min(VMEM_CAP_BYTES, nbytes * 5 // 4 + (6 << 20)))


def _params(n_axes, nbytes):
    return pltpu.CompilerParams(
        dimension_semantics=("arbitrary",) * n_axes,
        vmem_limit_bytes=_vmem_limit(nbytes))


def _const_spec(shape, single_buffer=False):
    mode = dict(pipeline_mode=pl.Buffered(1)) if single_buffer else {}
    return pl.BlockSpec(shape, lambda i: (0,) * len(shape), **mode)


def _sigmoid(x):
    return 1.0 / (1.0 + jnp.exp(-x))


def _gelu_tanh(x):
    c = math.sqrt(2.0 / math.pi)
    return x * (0.5 * (1.0 + jnp.tanh(c * (x + 0.044715 * (x * x * x)))))


def _layer_norm(v, g, b):
    mu = jnp.mean(v, axis=-1, keepdims=True)
    vc = v - mu
    var = jnp.mean(vc * vc, axis=-1, keepdims=True)
    return vc * lax.rsqrt(var + EPS) * g + b


def _rms_norm(x, g):
    ms = jnp.mean(x * x, axis=-1, keepdims=True)
    return x * lax.rsqrt(ms + EPS) * g


def _split_bf16(v):
    head = v.astype(BF16)
    return head, (v - head.astype(F32)).astype(BF16)


IN_PROJ_COLS = 1024


def _in_proj(x_ref, g_ref, w_ref, z_ref):
    h = _rms_norm(x_ref[...], g_ref[...]).astype(BF16)
    for c0 in range(0, w_ref.shape[1], IN_PROJ_COLS):
        cols = slice(c0, c0 + IN_PROJ_COLS)
        z_ref[:, cols] = jnp.dot(h, w_ref[:, cols], preferred_element_type=F32)


def _out_proj_route(x_ref, y_ref, w_ref, g_ref, wh_ref, wl_ref, br_ref, x1_ref, hn_ref, lg_ref):
    x1 = x_ref[...] + jnp.dot(y_ref[...], w_ref[...], preferred_element_type=F32)
    x1_ref[...] = x1
    hn = _rms_norm(x1, g_ref[...])
    hn_ref[...] = hn
    head, rem = _split_bf16(hn)
    lg_ref[...] = (jnp.dot(head, wh_ref[...], preferred_element_type=F32)
                   + jnp.dot(rem, wh_ref[...], preferred_element_type=F32)
                   + jnp.dot(head, wl_ref[...], preferred_element_type=F32)
                   + br_ref[...])


EVEN_HALO = SUBLANES


def _even_mix(z_ref, cw_ref, lg_ref, lb_ref, sw_ref, sb_ref, o_ref, pbuf):
    tm = z_ref.shape[0]
    kw, c = cw_ref.shape
    col = lambda j: z_ref[:, j * c:(j + 1) * c]

    @pl.when(pl.program_id(0) == 0)
    def _():
        pbuf[0:EVEN_HALO, :] = jnp.zeros((EVEN_HALO, c), F32)

    p = col(1) * col(2)
    pbuf[EVEN_HALO:EVEN_HALO + tm, :] = p
    conv = cw_ref[kw - 1:kw, :] * p
    for k in range(kw - 1):
        conv = conv + cw_ref[k:k + 1, :] * pbuf[pl.ds(EVEN_HALO - (kw - 1) + k, tm), :]
    o_ref[:, 0:c] = (col(0) * conv).astype(BF16)
    pbuf[0:EVEN_HALO, :] = pbuf[tm:tm + EVEN_HALO, :]

    u = _gelu_tanh(col(3))
    vn = _layer_norm(_gelu_tanh(col(4)), lg_ref[...], lb_ref[...]).astype(BF16)
    tpos = lax.broadcasted_iota(I32, (SGU_BLOCK, SGU_BLOCK), 0)
    spos = lax.broadcasted_iota(I32, (SGU_BLOCK, SGU_BLOCK), 1)
    mask = (spos // CHUNK) <= (tpos // CHUNK)
    hd = c // SGU_HEADS
    for h in range(SGU_HEADS):
        wh = jnp.where(mask, sw_ref[h], 0.0).astype(BF16)
        bh = sb_ref[:, h:h + 1]
        cols = slice(h * hd, (h + 1) * hd)
        for n in range(tm // SGU_BLOCK):
            rows = slice(n * SGU_BLOCK, (n + 1) * SGU_BLOCK)
            sv = jnp.dot(wh, vn[rows, cols], preferred_element_type=F32) + bh
            o_ref[rows, c + h * hd:c + (h + 1) * hd] = (u[rows, cols] * sv).astype(BF16)


POOL_HALO = 24
CONF_HALO = 32
CONV_ROWS = 128


def _window_sum(xbuf, cols, w, lv_a, lv_b, tm):
    lo, hi = SUBLANES, POOL_HALO + tm
    src, src_cols, n = xbuf, cols, 1
    dst, other = lv_a, lv_b
    while 2 * n < w:
        dst[lo:hi, :] = src[lo:hi, src_cols] + src[pl.ds(lo - n, hi - lo), src_cols]
        src, src_cols, n = dst, slice(None), 2 * n
        dst, other = other, dst
    return src[POOL_HALO:hi, src_cols] + src[pl.ds(POOL_HALO - n, tm), src_cols]


def _odd_mix(z_ref, pw_ref, ps_ref, dw_ref, db_ref, lg_ref, lb_ref, o_ref,
             xbuf, lv_a, lv_b, gbuf, sbuf, dbuf):
    tm = z_ref.shape[0]
    kw, c = dw_ref.shape
    i = pl.program_id(0)

    @pl.when(i == 0)
    def _():
        xbuf[0:POOL_HALO, :] = jnp.zeros((POOL_HALO, c), F32)
        gbuf[0:CONF_HALO, :] = jnp.zeros((CONF_HALO, c), F32)
        lv_a[0:SUBLANES, :] = jnp.zeros((SUBLANES, lv_a.shape[1]), F32)
        lv_b[0:SUBLANES, :] = jnp.zeros((SUBLANES, lv_b.shape[1]), F32)

    x = z_ref[:, 0:c]
    xbuf[POOL_HALO:POOL_HALO + tm, :] = x
    step = i * tm + lax.broadcasted_iota(I32, (tm, 1), 0) + 1
    gd = c // len(POOL_WINDOWS)
    for g, w in enumerate(POOL_WINDOWS):
        cols = slice(g * gd, (g + 1) * gd)
        s = _window_sum(xbuf, cols, w, lv_a, lv_b, tm)
        inv_count = 1.0 / jnp.minimum(step, w).astype(F32)
        d = s * inv_count - x[:, cols]
        y = jnp.dot(d.astype(BF16), pw_ref[g].astype(BF16), preferred_element_type=F32)
        o_ref[:, cols] = (y * ps_ref[:, cols]).astype(BF16)
    xbuf[0:POOL_HALO, :] = xbuf[tm:tm + POOL_HALO, :]

    gbuf[CONF_HALO:CONF_HALO + tm, :] = z_ref[:, c:2 * c] * _sigmoid(z_ref[:, 2 * c:3 * c])
    n_rows = CONF_HALO + tm

    def conv_cols(cb, carry):
        lanes = pl.ds(pl.multiple_of(cb * LANES, LANES), LANES)
        for b in range(1, SUBLANES):
            sbuf[b - 1, SUBLANES:n_rows, :] = gbuf[pl.ds(SUBLANES - b, n_rows - SUBLANES), lanes]
        accs = [jnp.broadcast_to(db_ref[:, lanes], (CONV_ROWS, LANES)) for _ in range(0, tm, CONV_ROWS)]
        for back in range(kw):
            a, b = divmod(back, SUBLANES)
            wk = dw_ref[kw - 1 - back:kw - back, lanes]
            for n, r0 in enumerate(range(0, tm, CONV_ROWS)):
                start = r0 + CONF_HALO - SUBLANES * a
                tap = gbuf[start:start + CONV_ROWS, lanes] if b == 0 else sbuf[b - 1, start:start + CONV_ROWS, :]
                accs[n] = accs[n] + wk * tap
        for n, r0 in enumerate(range(0, tm, CONV_ROWS)):
            dbuf[r0:r0 + CONV_ROWS, lanes] = accs[n]
        return carry

    lax.fori_loop(0, c // LANES, conv_cols, 0)
    gbuf[0:CONF_HALO, :] = gbuf[tm:tm + CONF_HALO, :]
    yn = _layer_norm(dbuf[...], lg_ref[...], lb_ref[...])
    o_ref[:, c:2 * c] = (yn * _sigmoid(yn)).astype(BF16)


def _start_row_gathers(pos_ref, ys_ref, ybuf, sem, slot):
    tm = ybuf.shape[2]
    for r in range(tm):
        for k in range(TOP_K):
            p = pos_ref[0, 0, k * tm + r]
            pltpu.make_async_copy(ys_ref.at[pl.ds(p, 1), :], ybuf.at[slot, k, pl.ds(r, 1), :],
                                  sem.at[slot]).start(priority=k)


def _wait_row_gathers(ys_ref, ybuf, sem, slot):
    tm = ybuf.shape[2]
    for k in range(TOP_K):
        _wait_rows(ys_ref.at[pl.ds(0, tm), :], ybuf.at[slot, k], sem.at[slot])


def _gathered_input(pos_cur, pos_next, x1_ref, gt_ref, ys_ref, x_ref, ybuf, sem):
    i = pl.program_id(0)
    slot = i % 2

    @pl.when(i == 0)
    def _():
        _start_row_gathers(pos_cur, ys_ref, ybuf, sem, 0)

    _wait_row_gathers(ys_ref, ybuf, sem, slot)
    x_ref[...] = x1_ref[...] + gt_ref[:, 0:1] * ybuf[slot, 0] + gt_ref[:, 1:2] * ybuf[slot, 1]
    _start_row_gathers(pos_next, ys_ref, ybuf, sem, 1 - slot)


def _front_body(*refs, mix, n_mix_params, gathered):
    n_head = 5 if gathered else 1
    head, refs = refs[:n_head], refs[n_head:]
    gm_ref, win_ref = refs[0:2]
    mix_params = refs[2:2 + n_mix_params]
    wout_ref, gf_ref, wh_ref, wl_ref, br_ref, x1_ref, hn_ref, lgt_ref, z_ref, y_ref = (
        refs[2 + n_mix_params:12 + n_mix_params])
    scratch = refs[12 + n_mix_params:]
    if gathered:
        mix_scratch, (x_ref, ybuf, sem) = scratch[:-3], scratch[-3:]
        _gathered_input(*head, x_ref, ybuf, sem)
    else:
        mix_scratch, x_ref = scratch, head[0]
    _in_proj(x_ref, gm_ref, win_ref, z_ref)
    mix(z_ref, *mix_params, y_ref, *mix_scratch)
    _out_proj_route(x_ref, y_ref, wout_ref, gf_ref, wh_ref, wl_ref, br_ref, x1_ref, hn_ref, lgt_ref)
    if gathered:
        @pl.when(pl.program_id(0) == pl.num_programs(0) - 1)
        def _():
            _wait_row_gathers(head[4], ybuf, sem, 1 - pl.program_id(0) % 2)


def _layer_front(name, mix, x_in, g_mix, w_in, mix_params, mix_scratch, w_out, g_ffn, wr, br, *, tm):
    gathered = isinstance(x_in, tuple)
    t, d = (x_in[0] if gathered else x_in).shape
    nt = t // tm
    n = w_in.shape[1]
    c2 = w_out.shape[0]
    wh, wl = _split_bf16(wr)
    row = lambda width: pl.BlockSpec((tm, width), lambda i: (i, 0))
    tail = [g_ffn.reshape(1, d), wh, wl, br]
    scratch = [pltpu.VMEM((tm, n), F32), pltpu.VMEM((tm, c2), BF16)] + mix_scratch
    if gathered:
        x1, gates_t, pos3, ys = x_in
        head = [pos3, pos3, x1, gates_t, ys]
        head_specs = [_pos_spec(tm),
                      pl.BlockSpec((1, 1, TOP_K * tm), lambda i: (jnp.minimum(i + 1, nt - 1), 0, 0),
                                   memory_space=pltpu.SMEM),
                      row(d), pl.BlockSpec((tm, TOP_K), lambda i: (i, 0)),
                      pl.BlockSpec(memory_space=pl.ANY)]
        scratch += [pltpu.VMEM((tm, d), F32), pltpu.VMEM((2, TOP_K, tm, d), F32),
                    pltpu.SemaphoreType.DMA((2,))]
    else:
        head, head_specs = [x_in], [row(d)]
    nbytes = (d * n * 2 + c2 * d * 2
              + 2 * 3 * tm * d * 4 + 2 * tm * LANES * 4
              + tm * n * 4 + tm * c2 * 2
              + sum(math.prod(s.shape) * 4 for s in mix_scratch)
              + 2 * sum(p.size * p.dtype.itemsize for p in mix_params + tail[1:])
              + (2 * TOP_K + 1) * tm * d * 4 * gathered
              + 6 * tm * d * 4)
    return pl.pallas_call(
        functools.partial(_front_body, mix=mix, n_mix_params=len(mix_params), gathered=gathered),
        grid=(nt,),
        in_specs=(head_specs + [_const_spec((1, d)), _const_spec(w_in.shape, single_buffer=True)]
                  + [_const_spec(p.shape) for p in mix_params]
                  + [_const_spec(w_out.shape, single_buffer=True), _const_spec((1, d)),
                     _const_spec(wh.shape), _const_spec(wl.shape), _const_spec(br.shape)]),
        out_specs=[row(d), row(d), row(LANES)],
        out_shape=[jax.ShapeDtypeStruct((t, d), F32),
                   jax.ShapeDtypeStruct((t, d), F32),
                   jax.ShapeDtypeStruct((t, LANES), F32)],
        scratch_shapes=scratch,
        compiler_params=_params(1, nbytes),
        name=name,
    )(*head, g_mix.reshape(1, d), w_in, *mix_params, w_out, *tail)


RANK_BLOCK = 256


def _route_block(logits, run, before):
    n = logits.shape[1]
    row8 = lax.broadcasted_iota(I32, (SUBLANES, n), 0)
    row = lax.broadcasted_iota(I32, (N_EXPERTS, n), 0)

    gl = jnp.where(row8 < N_GROUPS, logits[0:SUBLANES], NEG_BIG)
    gmax = jnp.max(gl, axis=0, keepdims=True)
    gidx = jnp.min(jnp.where(gl == gmax, row8, N_EXPERTS), axis=0, keepdims=True)
    p_group = 1.0 / jnp.sum(jnp.exp(gl - gmax), axis=0, keepdims=True)

    el = jnp.where(row // EXPERTS_PER_GROUP == gidx, logits[SUBLANES:ROUTER_ROWS], NEG_BIG)
    m0 = jnp.max(el, axis=0, keepdims=True)
    e0 = jnp.min(jnp.where(el == m0, row, N_EXPERTS), axis=0, keepdims=True)
    el1 = jnp.where(row == e0, NEG_BIG, el)
    m1 = jnp.max(el1, axis=0, keepdims=True)
    e1 = jnp.min(jnp.where(el1 == m1, row, N_EXPERTS), axis=0, keepdims=True)
    ratio = jnp.exp(m1 - m0)
    gate0 = p_group / (1.0 + ratio)
    gate1 = p_group * ratio / (1.0 + ratio)

    sel0 = row == e0
    sel1 = row == e1
    pre0 = jnp.dot(sel0.astype(BF16), before, preferred_element_type=F32)
    pre1 = jnp.dot(sel1.astype(BF16), before, preferred_element_type=F32)
    cnt0 = jnp.sum(sel0.astype(F32), axis=1, keepdims=True)
    cnt1 = jnp.sum(sel1.astype(F32), axis=1, keepdims=True)
    rank0 = jnp.sum(jnp.where(sel0, pre0 + run, 0.0), axis=0, keepdims=True)
    rank1 = jnp.sum(jnp.where(sel1, pre1 + (run + cnt0), 0.0), axis=0, keepdims=True)

    meta = jnp.where(row8 == 0, e0, jnp.where(row8 == 1, e1, jnp.where(
        row8 == 2, rank0.astype(I32), jnp.where(row8 == 3, rank1.astype(I32), 0))))
    gates = jnp.where(row8 == 0, gate0, jnp.where(row8 == 1, gate1, 0.0))
    return meta, gates, run + (cnt0 + cnt1)


def _route_body(lg_ref, meta_ref, gate_ref, cnt_ref, run_ref):
    tr = lg_ref.shape[0]
    nb = min(RANK_BLOCK, tr)

    @pl.when(pl.program_id(0) == 0)
    def _():
        run_ref[...] = jnp.zeros(run_ref.shape, F32)

    src = lax.broadcasted_iota(I32, (nb, nb), 0)
    dst = lax.broadcasted_iota(I32, (nb, nb), 1)
    before = (src < dst).astype(BF16)
    run = run_ref[:, 0:1]
    for b in range(tr // nb):
        cols = slice(b * nb, (b + 1) * nb)
        logits_t = lg_ref[cols, :].T[0:ROUTER_ROWS]
        meta, gates, run = _route_block(logits_t, run, before)
        meta_ref[:, cols] = meta
        gate_ref[:, cols] = gates
    run_ref[...] = jnp.broadcast_to(run, run_ref.shape)
    cnt_ref[...] = run_ref[...]


def _route(logits, *, tr):
    t = logits.shape[0]
    return pl.pallas_call(
        _route_body,
        grid=(t // tr,),
        in_specs=[pl.BlockSpec((tr, LANES), lambda i: (i, 0))],
        out_specs=[pl.BlockSpec((SUBLANES, tr), lambda i: (0, i)),
                   pl.BlockSpec((SUBLANES, tr), lambda i: (0, i)),
                   pl.BlockSpec((N_EXPERTS, LANES), lambda i: (0, 0))],
        out_shape=[jax.ShapeDtypeStruct((SUBLANES, t), I32),
                   jax.ShapeDtypeStruct((SUBLANES, t), F32),
                   jax.ShapeDtypeStruct((N_EXPERTS, LANES), F32)],
        scratch_shapes=[pltpu.VMEM((N_EXPERTS, LANES), F32)],
        compiler_params=_params(1, 64 * tr * 4 * 4),
        name="route",
    )(logits)


def _pos_spec(tm):
    return pl.BlockSpec((1, 1, TOP_K * tm), lambda i: (i, 0, 0), memory_space=pltpu.SMEM)


def _wait_rows(src_ref, dst_ref, sem):
    pltpu.make_async_copy(src_ref, dst_ref, sem).wait()


def _scatter_body(pos_ref, h_ref, xs_ref, sem):
    tm = h_ref.shape[0]

    for r in range(tm):
        for k in range(TOP_K):
            p = pos_ref[0, 0, k * tm + r]
            pltpu.make_async_copy(h_ref.at[pl.ds(r, 1), :], xs_ref.at[pl.ds(p, 1), :],
                                  sem).start(priority=k)
    for k in range(TOP_K):
        _wait_rows(h_ref, xs_ref.at[pl.ds(0, tm), :], sem)


def _scatter_rows(hn, pos3, *, tm):
    t, d = hn.shape
    return pl.pallas_call(
        _scatter_body,
        grid=(t // tm,),
        in_specs=[_pos_spec(tm), pl.BlockSpec((tm, d), lambda i: (i, 0))],
        out_specs=pl.BlockSpec(memory_space=pl.ANY),
        out_shape=jax.ShapeDtypeStruct((TOP_K * t, d), hn.dtype),
        scratch_shapes=[pltpu.SemaphoreType.DMA],
        compiler_params=_params(1, 2 * tm * d * 4),
        name="scatter_rows",
    )(pos3, hn)


def _expert_body(vblk, vexp, vfirst, vnew, vvalid, estart,
                 x_ref, w1_ref, w3_ref, w2_ref, o_ref, w1b, w3b, w2b):
    bm = x_ref.shape[0]
    v = pl.program_id(0)

    @pl.when(vvalid[v] == 1)
    def _():
        @pl.when(vnew[v] == 1)
        def _():
            w1b[...] = w1_ref[0, 0].astype(BF16)
            w3b[...] = w3_ref[0, 0].astype(BF16)
            w2b[...] = w2_ref[0, 0].astype(BF16)

        xb = x_ref[...].astype(BF16)
        h1 = jnp.dot(xb, w1b[...], preferred_element_type=F32)
        h3 = jnp.dot(xb, w3b[...], preferred_element_type=F32)
        hid = (h1 * _sigmoid(h1)) * h3
        y = jnp.dot(hid.astype(BF16), w2b[...], preferred_element_type=F32)

        @pl.when(vfirst[v] == 1)
        def _():
            o_ref[...] = y

        @pl.when(vfirst[v] == 0)
        def _():
            e = vexp[v]
            r = vblk[v] * bm + lax.broadcasted_iota(I32, (bm, 1), 0)
            mine = (r >= estart[e]) & (r < estart[e + 1])
            o_ref[...] = jnp.where(mine, y, o_ref[...])


def _experts(xs, w1, w3, w2, tables, *, layer, bm):
    n, d = xs.shape
    f = w1.shape[3]
    n_visits = tables[0].shape[0]
    nbytes = 2 * (bm * d * 4 * 2 + 3 * d * f * 4) + 3 * d * f * 2 + 6 * bm * d * 4
    wmap = lambda v, blk, exp, *_: (layer, exp[v], 0, 0)
    grid_spec = pltpu.PrefetchScalarGridSpec(
        num_scalar_prefetch=len(tables),
        grid=(n_visits,),
        in_specs=[pl.BlockSpec((bm, d), lambda v, blk, exp, *_: (blk[v], 0)),
                  pl.BlockSpec((1, 1, d, f), wmap),
                  pl.BlockSpec((1, 1, d, f), wmap),
                  pl.BlockSpec((1, 1, f, d), wmap)],
        out_specs=pl.BlockSpec((bm, d), lambda v, blk, exp, *_: (blk[v], 0)),
        scratch_shapes=[pltpu.VMEM((d, f), BF16), pltpu.VMEM((d, f), BF16),
                        pltpu.VMEM((f, d), BF16)])
    return pl.pallas_call(
        _expert_body,
        grid_spec=grid_spec,
        out_shape=jax.ShapeDtypeStruct((n, d), F32),
        compiler_params=_params(1, nbytes),
        name="experts",
    )(*tables, xs, w1, w3, w2)


def _visit_tables(counts, n_slots, bm):
    n_blocks = n_slots // bm
    n_visits = n_blocks + N_EXPERTS - 1
    end = jnp.cumsum(counts)
    start = end - counts
    first_blk = start // bm
    last_blk = jnp.maximum(end - 1, 0) // bm
    per_e = jnp.where(counts > 0, last_blk - first_blk + 1, 0)
    cum = jnp.cumsum(per_e)
    total = cum[-1]
    v = jnp.minimum(jnp.arange(n_visits, dtype=I32), total - 1)
    e = jnp.sum((cum[None, :] <= v[:, None]).astype(I32), axis=1)
    blk = (first_blk[e] + (v - (cum[e] - per_e[e]))).astype(I32)
    valid = (jnp.arange(n_visits) < total).astype(I32)
    prev_blk = jnp.concatenate([jnp.full((1,), -1, I32), blk[:-1]])
    prev_e = jnp.concatenate([jnp.full((1,), -1, I32), e[:-1]])
    first = (blk != prev_blk).astype(I32)
    new = (e != prev_e).astype(I32)
    estart = jnp.concatenate([start, end[-1:]]).astype(I32)
    return blk, e, first, new, valid, estart


def _combine_body(pos_ref, x_ref, gt_ref, g_ref, ys_ref, o_ref, ybuf, sem):
    tm = x_ref.shape[0]
    for r in range(tm):
        for k in range(TOP_K):
            p = pos_ref[0, 0, k * tm + r]
            pltpu.make_async_copy(ys_ref.at[pl.ds(p, 1), :], ybuf.at[k, pl.ds(r, 1), :],
                                  sem).start(priority=k)
    for k in range(TOP_K):
        _wait_rows(ys_ref.at[pl.ds(0, tm), :], ybuf.at[k], sem)
    x2 = x_ref[...] + gt_ref[:, 0:1] * ybuf[0] + gt_ref[:, 1:2] * ybuf[1]
    o_ref[...] = _rms_norm(x2, g_ref[...])


def _combine(x1, ys, pos3, gates_t, g_final, *, tm):
    t, d = x1.shape
    nbytes = 2 * (2 * tm * d * 4 + tm * LANES * 4) + TOP_K * tm * d * 4 + 2 * tm * d * 4
    return pl.pallas_call(
        _combine_body,
        grid=(t // tm,),
        in_specs=[_pos_spec(tm),
                  pl.BlockSpec((tm, d), lambda i: (i, 0)),
                  pl.BlockSpec((tm, TOP_K), lambda i: (i, 0)),
                  pl.BlockSpec((1, d), lambda i: (0, 0)),
                  pl.BlockSpec(memory_space=pl.ANY)],
        out_specs=pl.BlockSpec((tm, d), lambda i: (i, 0)),
        out_shape=jax.ShapeDtypeStruct((t, d), F32),
        scratch_shapes=[pltpu.VMEM((TOP_K, tm, d), F32), pltpu.SemaphoreType.DMA],
        compiler_params=_params(1, nbytes),
        name="combine",
    )(pos3, x1, gates_t, g_final.reshape(1, d), ys)


def _tiles(t):
    return dict(tm_front=min(256, t), tr=min(2048, t), tm_rows=min(256, t), bm=min(256, TOP_K * t))


def _router_params(w_group, b_group, w_router, b_router):
    d = w_group.shape[0]
    gpad = SUBLANES - N_GROUPS
    wr = jnp.concatenate([w_group, jnp.zeros((d, gpad), F32), w_router,
                          jnp.zeros((d, LANES - ROUTER_ROWS), F32)], axis=1)
    br = jnp.concatenate([b_group, jnp.zeros((gpad,), F32), b_router,
                          jnp.zeros((LANES - ROUTER_ROWS,), F32)]).reshape(1, LANES)
    return wr, br


def _moe(x1, hn, logits, w1, w3, w2, layer, tl):
    t, d = x1.shape
    meta, gates, cnt = _route(logits, tr=tl["tr"])

    counts = cnt[:, 0].astype(I32)
    start = jnp.cumsum(counts) - counts
    is_e = meta[0:TOP_K, :, None] == jnp.arange(N_EXPERTS, dtype=I32)
    pos = jnp.sum(jnp.where(is_e, start, 0), axis=-1) + meta[TOP_K:2 * TOP_K]
    tm = tl["tm_rows"]
    pos3 = pos.reshape(TOP_K, t // tm, tm).transpose(1, 0, 2).reshape(t // tm, 1, TOP_K * tm)

    xs = _scatter_rows(hn, pos3, tm=tm)
    tables = _visit_tables(counts, TOP_K * t, tl["bm"])
    ys = _experts(xs, w1, w3, w2, tables, layer=layer, bm=tl["bm"])
    return x1, gates[0:TOP_K].T, pos3, ys


def kernel(x, ln_mix, ln_ffn, ln_final, ev_w_in, ev_conv_w, ev_sgu_ln_g, ev_sgu_ln_b, ev_sgu_w, ev_sgu_b, ev_w_out, od_w_in, od_pool_w, od_pool_scale, od_dw_w, od_dw_b, od_ln_g, od_ln_b, od_w_out, moe_w_group, moe_b_group, moe_w_router, moe_b_router, moe_w1, moe_w3, moe_w2):
    bsz, s, d = x.shape
    assert bsz == 1, "token mixers carry state across row tiles of one sequence"
    depth = ln_mix.shape[0]
    tl = _tiles(s)
    tm = tl["tm_front"]
    assert tm == tl["tm_rows"], "the fused gather reads pos3 in front-kernel tiles"
    x_in = x.reshape(s, d)
    for layer in range(depth):
        i = layer // 2
        wr, br = _router_params(moe_w_group[layer], moe_b_group[layer],
                                moe_w_router[layer], moe_b_router[layer])
        if layer % 2 == 0:
            c = ev_conv_w.shape[2]
            mix_params = [ev_conv_w[i], ev_sgu_ln_g[i].reshape(1, c), ev_sgu_ln_b[i].reshape(1, c),
                          ev_sgu_w[i], ev_sgu_b[i].T]
            mix_scratch = [pltpu.VMEM((tm + EVEN_HALO, c), F32)]
            x1, hn, logits = _layer_front(
                "even_layer", _even_mix, x_in, ln_mix[layer], ev_w_in[i].astype(BF16), mix_params,
                mix_scratch, ev_w_out[i].astype(BF16), ln_ffn[layer], wr, br, tm=tm)
        else:
            c = od_dw_w.shape[2]
            mix_params = [od_pool_w[i], od_pool_scale[i].reshape(1, c), od_dw_w[i],
                          od_dw_b[i].reshape(1, c), od_ln_g[i].reshape(1, c), od_ln_b[i].reshape(1, c)]
            gd = c // len(POOL_WINDOWS)
            mix_scratch = [pltpu.VMEM((tm + POOL_HALO, c), F32),
                           pltpu.VMEM((tm + POOL_HALO, gd), F32), pltpu.VMEM((tm + POOL_HALO, gd), F32),
                           pltpu.VMEM((tm + CONF_HALO, c), F32),
                           pltpu.VMEM((SUBLANES - 1, tm + CONF_HALO, LANES), F32),
                           pltpu.VMEM((tm, c), F32)]
            x1, hn, logits = _layer_front(
                "odd_layer", _odd_mix, x_in, ln_mix[layer], od_w_in[i].astype(BF16), mix_params,
                mix_scratch, od_w_out[i].astype(BF16), ln_ffn[layer], wr, br, tm=tm)
        x_in = _moe(x1, hn, logits, moe_w1, moe_w3, moe_w2, layer, tl)
    x1, gates_t, pos3, ys = x_in
    out = _combine(x1, ys, pos3, gates_t, ln_final, tm=tl["tm_rows"])
    return out.reshape(bsz, s, d)
```

```python
import functools
import math

import jax
import jax.numpy as jnp
from jax import lax
from jax.experimental import pallas as pl
from jax.experimental.pallas import tpu as pltpu

F32 = jnp.float32
BF16 = jnp.bfloat16
I32 = jnp.int32

EPS = 1e-6
CHUNK = 64
SGU_BLOCK = 128
SGU_HEADS = 8
POOL_WINDOWS = (2, 4, 8, 16)
N_GROUPS = 4
EXPERTS_PER_GROUP = 8
N_EXPERTS = N_GROUPS * EXPERTS_PER_GROUP
TOP_K = 2

LANES = 128
SUBLANES = 8
VMEM_CAP_BYTES = 60 * 1024 * 1024

ROUTER_ROWS = SUBLANES + N_EXPERTS
NEG_BIG = -1e30


def _vmem_limit(nbytes):
    return int(min(VMEM_CAP_BYTES, nbytes * 5 // 4 + (6 << 20)))


def _params(n_axes, nbytes):
    return pltpu.CompilerParams(
        dimension_semantics=("arbitrary",) * n_axes,
        vmem_limit_bytes=_vmem_limit(nbytes))


def _const_spec(shape, single_buffer=False):
    mode = dict(pipeline_mode=pl.Buffered(1)) if single_buffer else {}
    return pl.BlockSpec(shape, lambda i: (0,) * len(shape), **mode)


def _sigmoid(x):
    return 1.0 / (1.0 + jnp.exp(-x))


def _gelu_tanh(x):
    c = math.sqrt(2.0 / math.pi)
    return x * (0.5 * (1.0 + jnp.tanh(c * (x + 0.044715 * (x * x * x)))))


def _layer_norm(v, g, b):
    mu = jnp.mean(v, axis=-1, keepdims=True)
    vc = v - mu
    var = jnp.mean(vc * vc, axis=-1, keepdims=True)
    return vc * lax.rsqrt(var + EPS) * g + b


def _rms_norm(x, g):
    ms = jnp.mean(x * x, axis=-1, keepdims=True)
    return x * lax.rsqrt(ms + EPS) * g


def _split_bf16(v):
    head = v.astype(BF16)
    return head, (v - head.astype(F32)).astype(BF16)


IN_PROJ_COLS = 1024


def _in_proj(x_ref, g_ref, w_ref, z_ref):
    h = _rms_norm(x_ref[...], g_ref[...]).astype(BF16)
    for c0 in range(0, w_ref.shape[1], IN_PROJ_COLS):
        cols = slice(c0, c0 + IN_PROJ_COLS)
        z_ref[:, cols] = jnp.dot(h, w_ref[:, cols], preferred_element_type=F32)


def _out_proj_route(x_ref, y_ref, w_ref, g_ref, wh_ref, wl_ref, br_ref, x1_ref, hn_ref, lg_ref):
    x1 = x_ref[...] + jnp.dot(y_ref[...], w_ref[...], preferred_element_type=F32)
    x1_ref[...] = x1
    hn = _rms_norm(x1, g_ref[...])
    hn_ref[...] = hn
    head, rem = _split_bf16(hn)
    lg_ref[...] = (jnp.dot(head, wh_ref[...], preferred_element_type=F32)
                   + jnp.dot(rem, wh_ref[...], preferred_element_type=F32)
                   + jnp.dot(head, wl_ref[...], preferred_element_type=F32)
                   + br_ref[...])


EVEN_HALO = SUBLANES


def _even_mix(z_ref, cw_ref, lg_ref, lb_ref, sw_ref, sb_ref, o_ref, pbuf):
    tm = z_ref.shape[0]
    kw, c = cw_ref.shape
    col = lambda j: z_ref[:, j * c:(j + 1) * c]

    @pl.when(pl.program_id(0) == 0)
    def _():
        pbuf[0:EVEN_HALO, :] = jnp.zeros((EVEN_HALO, c), F32)

    p = col(1) * col(2)
    pbuf[EVEN_HALO:EVEN_HALO + tm, :] = p
    conv = cw_ref[kw - 1:kw, :] * p
    for k in range(kw - 1):
        conv = conv + cw_ref[k:k + 1, :] * pbuf[pl.ds(EVEN_HALO - (kw - 1) + k, tm), :]
    o_ref[:, 0:c] = (col(0) * conv).astype(BF16)
    pbuf[0:EVEN_HALO, :] = pbuf[tm:tm + EVEN_HALO, :]

    u = _gelu_tanh(col(3))
    vn = _layer_norm(_gelu_tanh(col(4)), lg_ref[...], lb_ref[...]).astype(BF16)
    tpos = lax.broadcasted_iota(I32, (SGU_BLOCK, SGU_BLOCK), 0)
    spos = lax.broadcasted_iota(I32, (SGU_BLOCK, SGU_BLOCK), 1)
    mask = (spos // CHUNK) <= (tpos // CHUNK)
    hd = c // SGU_HEADS
    for h in range(SGU_HEADS):
        wh = jnp.where(mask, sw_ref[h], 0.0).astype(BF16)
        bh = sb_ref[:, h:h + 1]
        cols = slice(h * hd, (h + 1) * hd)
        for n in range(tm // SGU_BLOCK):
            rows = slice(n * SGU_BLOCK, (n + 1) * SGU_BLOCK)
            sv = jnp.dot(wh, vn[rows, cols], preferred_element_type=F32) + bh
            o_ref[rows, c + h * hd:c + (h + 1) * hd] = (u[rows, cols] * sv).astype(BF16)


POOL_HALO = 24
CONF_HALO = 32
CONV_ROWS = 128


def _window_sum(xbuf, cols, w, lv_a, lv_b, tm):
    lo, hi = SUBLANES, POOL_HALO + tm
    src, src_cols, n = xbuf, cols, 1
    dst, other = lv_a, lv_b
    while 2 * n < w:
        dst[lo:hi, :] = src[lo:hi, src_cols] + src[pl.ds(lo - n, hi - lo), src_cols]
        src, src_cols, n = dst, slice(None), 2 * n
        dst, other = other, dst
    return src[POOL_HALO:hi, src_cols] + src[pl.ds(POOL_HALO - n, tm), src_cols]


def _odd_mix(z_ref, pw_ref, ps_ref, dw_ref, db_ref, lg_ref, lb_ref, o_ref,
             xbuf, lv_a, lv_b, gbuf, sbuf, dbuf):
    tm = z_ref.shape[0]
    kw, c = dw_ref.shape
    i = pl.program_id(0)

    @pl.when(i == 0)
    def _():
        xbuf[0:POOL_HALO, :] = jnp.zeros((POOL_HALO, c), F32)
        gbuf[0:CONF_HALO, :] = jnp.zeros((CONF_HALO, c), F32)
        lv_a[0:SUBLANES, :] = jnp.zeros((SUBLANES, lv_a.shape[1]), F32)
        lv_b[0:SUBLANES, :] = jnp.zeros((SUBLANES, lv_b.shape[1]), F32)

    x = z_ref[:, 0:c]
    xbuf[POOL_HALO:POOL_HALO + tm, :] = x
    step = i * tm + lax.broadcasted_iota(I32, (tm, 1), 0) + 1
    gd = c // len(POOL_WINDOWS)
    for g, w in enumerate(POOL_WINDOWS):
        cols = slice(g * gd, (g + 1) * gd)
        s = _window_sum(xbuf, cols, w, lv_a, lv_b, tm)
        inv_count = 1.0 / jnp.minimum(step, w).astype(F32)
        d = s * inv_count - x[:, cols]
        y = jnp.dot(d.astype(BF16), pw_ref[g].astype(BF16), preferred_element_type=F32)
        o_ref[:, cols] = (y * ps_ref[:, cols]).astype(BF16)
    xbuf[0:POOL_HALO, :] = xbuf[tm:tm + POOL_HALO, :]

    gbuf[CONF_HALO:CONF_HALO + tm, :] = z_ref[:, c:2 * c] * _sigmoid(z_ref[:, 2 * c:3 * c])
    n_rows = CONF_HALO + tm

    def conv_cols(cb, carry):
        lanes = pl.ds(pl.multiple_of(cb * LANES, LANES), LANES)
        for b in range(1, SUBLANES):
            sbuf[b - 1, SUBLANES:n_rows, :] = gbuf[pl.ds(SUBLANES - b, n_rows - SUBLANES), lanes]
        accs = [jnp.broadcast_to(db_ref[:, lanes], (CONV_ROWS, LANES)) for _ in range(0, tm, CONV_ROWS)]
        for back in range(kw):
            a, b = divmod(back, SUBLANES)
            wk = dw_ref[kw - 1 - back:kw - back, lanes]
            for n, r0 in enumerate(range(0, tm, CONV_ROWS)):
                start = r0 + CONF_HALO - SUBLANES * a
                tap = gbuf[start:start + CONV_ROWS, lanes] if b == 0 else sbuf[b - 1, start:start + CONV_ROWS, :]
                accs[n] = accs[n] + wk * tap
        for n, r0 in enumerate(range(0, tm, CONV_ROWS)):
            dbuf[r0:r0 + CONV_ROWS, lanes] = accs[n]
        return carry

    lax.fori_loop(0, c // LANES, conv_cols, 0)
    gbuf[0:CONF_HALO, :] = gbuf[tm:tm + CONF_HALO, :]
    yn = _layer_norm(dbuf[...], lg_ref[...], lb_ref[...])
    o_ref[:, c:2 * c] = (yn * _sigmoid(yn)).astype(BF16)


def _start_row_gathers(pos_ref, ys_ref, ybuf, sem, slot):
    tm = ybuf.shape[2]
    for r in range(tm):
        for k in range(TOP_K):
            p = pos_ref[0, 0, k * tm + r]
            pltpu.make_async_copy(ys_ref.at[pl.ds(p, 1), :], ybuf.at[slot, k, pl.ds(r, 1), :],
                                  sem.at[slot]).start(priority=k)


def _wait_row_gathers(ys_ref, ybuf, sem, slot):
    tm = ybuf.shape[2]
    for k in range(TOP_K):
        _wait_rows(ys_ref.at[pl.ds(0, tm), :], ybuf.at[slot, k], sem.at[slot])


def _gathered_input(pos_cur, pos_next, x1_ref, gt_ref, ys_ref, x_ref, ybuf, sem):
    i = pl.program_id(0)
    slot = i % 2

    @pl.when(i == 0)
    def _():
        _start_row_gathers(pos_cur, ys_ref, ybuf, sem, 0)

    _wait_row_gathers(ys_ref, ybuf, sem, slot)
    x_ref[...] = x1_ref[...] + gt_ref[:, 0:1] * ybuf[slot, 0] + gt_ref[:, 1:2] * ybuf[slot, 1]
    _start_row_gathers(pos_next, ys_ref, ybuf, sem, 1 - slot)


def _front_body(*refs, mix, n_mix_params, gathered):
    n_head = 5 if gathered else 1
    head, refs = refs[:n_head], refs[n_head:]
    gm_ref, win_ref = refs[0:2]
    mix_params = refs[2:2 + n_mix_params]
    wout_ref, gf_ref, wh_ref, wl_ref, br_ref, x1_ref, hn_ref, lgt_ref, z_ref, y_ref = (
        refs[2 + n_mix_params:12 + n_mix_params])
    scratch = refs[12 + n_mix_params:]
    if gathered:
        mix_scratch, (x_ref, ybuf, sem) = scratch[:-3], scratch[-3:]
        _gathered_input(*head, x_ref, ybuf, sem)
    else:
        mix_scratch, x_ref = scratch, head[0]
    _in_proj(x_ref, gm_ref, win_ref, z_ref)
    mix(z_ref, *mix_params, y_ref, *mix_scratch)
    _out_proj_route(x_ref, y_ref, wout_ref, gf_ref, wh_ref, wl_ref, br_ref, x1_ref, hn_ref, lgt_ref)
    if gathered:
        @pl.when(pl.program_id(0) == pl.num_programs(0) - 1)
        def _():
            _wait_row_gathers(head[4], ybuf, sem, 1 - pl.program_id(0) % 2)


def _layer_front(name, mix, x_in, g_mix, w_in, mix_params, mix_scratch, w_out, g_ffn, wr, br, *, tm):
    gathered = isinstance(x_in, tuple)
    t, d = (x_in[0] if gathered else x_in).shape
    nt = t // tm
    n = w_in.shape[1]
    c2 = w_out.shape[0]
    wh, wl = _split_bf16(wr)
    row = lambda width: pl.BlockSpec((tm, width), lambda i: (i, 0))
    tail = [g_ffn.reshape(1, d), wh, wl, br]
    scratch = [pltpu.VMEM((tm, n), F32), pltpu.VMEM((tm, c2), BF16)] + mix_scratch
    if gathered:
        x1, gates_t, pos3, ys = x_in
        head = [pos3, pos3, x1, gates_t, ys]
        head_specs = [_pos_spec(tm),
                      pl.BlockSpec((1, 1, TOP_K * tm), lambda i: (jnp.minimum(i + 1, nt - 1), 0, 0),
                                   memory_space=pltpu.SMEM),
                      row(d), pl.BlockSpec((tm, TOP_K), lambda i: (i, 0)),
                      pl.BlockSpec(memory_space=pl.ANY)]
        scratch += [pltpu.VMEM((tm, d), F32), pltpu.VMEM((2, TOP_K, tm, d), F32),
                    pltpu.SemaphoreType.DMA((2,))]
    else:
        head, head_specs = [x_in], [row(d)]
    nbytes = (d * n * 2 + c2 * d * 2
              + 2 * 3 * tm * d * 4 + 2 * tm * LANES * 4
              + tm * n * 4 + tm * c2 * 2
              + sum(math.prod(s.shape) * 4 for s in mix_scratch)
              + 2 * sum(p.size * p.dtype.itemsize for p in mix_params + tail[1:])
              + (2 * TOP_K + 1) * tm * d * 4 * gathered
              + 6 * tm * d * 4)
    return pl.pallas_call(
        functools.partial(_front_body, mix=mix, n_mix_params=len(mix_params), gathered=gathered),
        grid=(nt,),
        in_specs=(head_specs + [_const_spec((1, d)), _const_spec(w_in.shape, single_buffer=True)]
                  + [_const_spec(p.shape) for p in mix_params]
                  + [_const_spec(w_out.shape, single_buffer=True), _const_spec((1, d)),
                     _const_spec(wh.shape), _const_spec(wl.shape), _const_spec(br.shape)]),
        out_specs=[row(d), row(d), row(LANES)],
        out_shape=[jax.ShapeDtypeStruct((t, d), F32),
                   jax.ShapeDtypeStruct((t, d), F32),
                   jax.ShapeDtypeStruct((t, LANES), F32)],
        scratch_shapes=scratch,
        compiler_params=_params(1, nbytes),
        name=name,
    )(*head, g_mix.reshape(1, d), w_in, *mix_params, w_out, *tail)


RANK_BLOCK = 256


def _route_block(logits, run, before):
    n = logits.shape[1]
    row8 = lax.broadcasted_iota(I32, (SUBLANES, n), 0)
    row = lax.broadcasted_iota(I32, (N_EXPERTS, n), 0)

    gl = jnp.where(row8 < N_GROUPS, logits[0:SUBLANES], NEG_BIG)
    gmax = jnp.max(gl, axis=0, keepdims=True)
    gidx = jnp.min(jnp.where(gl == gmax, row8, N_EXPERTS), axis=0, keepdims=True)
    p_group = 1.0 / jnp.sum(jnp.exp(gl - gmax), axis=0, keepdims=True)

    el = jnp.where(row // EXPERTS_PER_GROUP == gidx, logits[SUBLANES:ROUTER_ROWS], NEG_BIG)
    m0 = jnp.max(el, axis=0, keepdims=True)
    e0 = jnp.min(jnp.where(el == m0, row, N_EXPERTS), axis=0, keepdims=True)
    el1 = jnp.where(row == e0, NEG_BIG, el)
    m1 = jnp.max(el1, axis=0, keepdims=True)
    e1 = jnp.min(jnp.where(el1 == m1, row, N_EXPERTS), axis=0, keepdims=True)
    ratio = jnp.exp(m1 - m0)
    gate0 = p_group / (1.0 + ratio)
    gate1 = p_group * ratio / (1.0 + ratio)

    sel0 = row == e0
    sel1 = row == e1
    pre0 = jnp.dot(sel0.astype(BF16), before, preferred_element_type=F32)
    pre1 = jnp.dot(sel1.astype(BF16), before, preferred_element_type=F32)
    cnt0 = jnp.sum(sel0.astype(F32), axis=1, keepdims=True)
    cnt1 = jnp.sum(sel1.astype(F32), axis=1, keepdims=True)
    rank0 = jnp.sum(jnp.where(sel0, pre0 + run, 0.0), axis=0, keepdims=True)
    rank1 = jnp.sum(jnp.where(sel1, pre1 + (run + cnt0), 0.0), axis=0, keepdims=True)

    meta = jnp.where(row8 == 0, e0, jnp.where(row8 == 1, e1, jnp.where(
        row8 == 2, rank0.astype(I32), jnp.where(row8 == 3, rank1.astype(I32), 0))))
    gates = jnp.where(row8 == 0, gate0, jnp.where(row8 == 1, gate1, 0.0))
    return meta, gates, run + (cnt0 + cnt1)


def _route_body(lg_ref, meta_ref, gate_ref, cnt_ref, run_ref):
    tr = lg_ref.shape[0]
    nb = min(RANK_BLOCK, tr)

    @pl.when(pl.program_id(0) == 0)
    def _():
        run_ref[...] = jnp.zeros(run_ref.shape, F32)

    src = lax.broadcasted_iota(I32, (nb, nb), 0)
    dst = lax.broadcasted_iota(I32, (nb, nb), 1)
    before = (src < dst).astype(BF16)
    run = run_ref[:, 0:1]
    for b in range(tr // nb):
        cols = slice(b * nb, (b + 1) * nb)
        logits_t = lg_ref[cols, :].T[0:ROUTER_ROWS]
        meta, gates, run = _route_block(logits_t, run, before)
        meta_ref[:, cols] = meta
        gate_ref[:, cols] = gates
    run_ref[...] = jnp.broadcast_to(run, run_ref.shape)
    cnt_ref[...] = run_ref[...]


def _route(logits, *, tr):
    t = logits.shape[0]
    return pl.pallas_call(
        _route_body,
        grid=(t // tr,),
        in_specs=[pl.BlockSpec((tr, LANES), lambda i: (i, 0))],
        out_specs=[pl.BlockSpec((SUBLANES, tr), lambda i: (0, i)),
                   pl.BlockSpec((SUBLANES, tr), lambda i: (0, i)),
                   pl.BlockSpec((N_EXPERTS, LANES), lambda i: (0, 0))],
        out_shape=[jax.ShapeDtypeStruct((SUBLANES, t), I32),
                   jax.ShapeDtypeStruct((SUBLANES, t), F32),
                   jax.ShapeDtypeStruct((N_EXPERTS, LANES), F32)],
        scratch_shapes=[pltpu.VMEM((N_EXPERTS, LANES), F32)],
        compiler_params=_params(1, 64 * tr * 4 * 4),
        name="route",
    )(logits)


def _pos_spec(tm):
    return pl.BlockSpec((1, 1, TOP_K * tm), lambda i: (i, 0, 0), memory_space=pltpu.SMEM)


def _wait_rows(src_ref, dst_ref, sem):
    pltpu.make_async_copy(src_ref, dst_ref, sem).wait()


def _scatter_body(pos_ref, h_ref, xs_ref, sem):
    tm = h_ref.shape[0]

    for r in range(tm):
        for k in range(TOP_K):
            p = pos_ref[0, 0, k * tm + r]
            pltpu.make_async_copy(h_ref.at[pl.ds(r, 1), :], xs_ref.at[pl.ds(p, 1), :],
                                  sem).start(priority=k)
    for k in range(TOP_K):
        _wait_rows(h_ref, xs_ref.at[pl.ds(0, tm), :], sem)


def _scatter_rows(hn, pos3, *, tm):
    t, d = hn.shape
    return pl.pallas_call(
        _scatter_body,
        grid=(t // tm,),
        in_specs=[_pos_spec(tm), pl.BlockSpec((tm, d), lambda i: (i, 0))],
        out_specs=pl.BlockSpec(memory_space=pl.ANY),
        out_shape=jax.ShapeDtypeStruct((TOP_K * t, d), hn.dtype),
        scratch_shapes=[pltpu.SemaphoreType.DMA],
        compiler_params=_params(1, 2 * tm * d * 4),
        name="scatter_rows",
    )(pos3, hn)


def _expert_body(vblk, vexp, vfirst, vnew, vvalid, vslot, vnext, estart,
                 x_ref, w1_hbm, w3_hbm, w2_hbm, o_ref, wf1, wf3, wf2, w1b, w3b, w2b, sem, *, layer):
    bm = x_ref.shape[0]
    v = pl.program_id(0)
    weights = ((w1_hbm, wf1), (w3_hbm, wf3), (w2_hbm, wf2))

    def weight_copies(e, slot):
        return [pltpu.make_async_copy(w_hbm.at[layer, e], wf.at[slot], sem.at[slot, j])
                for j, (w_hbm, wf) in enumerate(weights)]

    @pl.when(v == 0)
    def _():
        for c in weight_copies(vexp[0], 0):
            c.start()

    @pl.when(vvalid[v] == 1)
    def _():
        @pl.when(vnew[v] == 1)
        def _():
            slot = vslot[v]
            for c in weight_copies(vexp[v], slot):
                c.wait()

            @pl.when(vnext[v] >= 0)
            def _():
                for c in weight_copies(vnext[v], 1 - slot):
                    c.start()

            w1b[...] = wf1[slot].astype(BF16)
            w3b[...] = wf3[slot].astype(BF16)
            w2b[...] = wf2[slot].astype(BF16)

        xb = x_ref[...].astype(BF16)
        h1 = jnp.dot(xb, w1b[...], preferred_element_type=F32)
        h3 = jnp.dot(xb, w3b[...], preferred_element_type=F32)
        hid = (h1 * _sigmoid(h1)) * h3
        y = jnp.dot(hid.astype(BF16), w2b[...], preferred_element_type=F32)

        @pl.when(vfirst[v] == 1)
        def _():
            o_ref[...] = y

        @pl.when(vfirst[v] == 0)
        def _():
            e = vexp[v]
            r = vblk[v] * bm + lax.broadcasted_iota(I32, (bm, 1), 0)
            mine = (r >= estart[e]) & (r < estart[e + 1])
            o_ref[...] = jnp.where(mine, y, o_ref[...])


def _experts(xs, w1, w3, w2, tables, *, layer, bm):
    n, d = xs.shape
    f = w1.shape[3]
    n_visits = tables[0].shape[0]
    nbytes = 2 * (bm * d * 4 * 2 + 3 * d * f * 4) + 3 * d * f * 2 + 6 * bm * d * 4
    hbm = pl.BlockSpec(memory_space=pl.ANY)
    grid_spec = pltpu.PrefetchScalarGridSpec(
        num_scalar_prefetch=len(tables),
        grid=(n_visits,),
        in_specs=[pl.BlockSpec((bm, d), lambda v, blk, *_: (blk[v], 0)), hbm, hbm, hbm],
        out_specs=pl.BlockSpec((bm, d), lambda v, blk, *_: (blk[v], 0)),
        scratch_shapes=[pltpu.VMEM((2, d, f), F32), pltpu.VMEM((2, d, f), F32), pltpu.VMEM((2, f, d), F32),
                        pltpu.VMEM((d, f), BF16), pltpu.VMEM((d, f), BF16), pltpu.VMEM((f, d), BF16),
                        pltpu.SemaphoreType.DMA((2, 3))])
    return pl.pallas_call(
        functools.partial(_expert_body, layer=layer),
        grid_spec=grid_spec,
        out_shape=jax.ShapeDtypeStruct((n, d), F32),
        compiler_params=_params(1, nbytes),
        name="experts",
    )(*tables, xs, w1, w3, w2)


def _visit_tables(counts, n_slots, bm):
    n_blocks = n_slots // bm
    n_visits = n_blocks + N_EXPERTS - 1
    end = jnp.cumsum(counts)
    start = end - counts
    first_blk = start // bm
    last_blk = jnp.maximum(end - 1, 0) // bm
    per_e = jnp.where(counts > 0, last_blk - first_blk + 1, 0)
    cum = jnp.cumsum(per_e)
    total = cum[-1]
    v = jnp.minimum(jnp.arange(n_visits, dtype=I32), total - 1)
    e = jnp.sum((cum[None, :] <= v[:, None]).astype(I32), axis=1)
    blk = (first_blk[e] + (v - (cum[e] - per_e[e]))).astype(I32)
    valid = (jnp.arange(n_visits) < total).astype(I32)
    prev_blk = jnp.concatenate([jnp.full((1,), -1, I32), blk[:-1]])
    prev_e = jnp.concatenate([jnp.full((1,), -1, I32), e[:-1]])
    first = (blk != prev_blk).astype(I32)
    new = (e != prev_e).astype(I32)
    ordinal = jnp.cumsum(new) - 1
    slot = (ordinal % 2).astype(I32)
    is_next = (new[None, :] == 1) & (ordinal[None, :] == ordinal[:, None] + 1)
    nxt = jnp.where(jnp.any(is_next, axis=1), jnp.sum(jnp.where(is_next, e[None, :], 0), axis=1), -1)
    estart = jnp.concatenate([start, end[-1:]]).astype(I32)
    return blk, e, first, new, valid, slot, nxt.astype(I32), estart


def _combine_body(pos_ref, x_ref, gt_ref, g_ref, ys_ref, o_ref, ybuf, sem):
    tm = x_ref.shape[0]
    for r in range(tm):
        for k in range(TOP_K):
            p = pos_ref[0, 0, k * tm + r]
            pltpu.make_async_copy(ys_ref.at[pl.ds(p, 1), :], ybuf.at[k, pl.ds(r, 1), :],
                                  sem).start(priority=k)
    for k in range(TOP_K):
        _wait_rows(ys_ref.at[pl.ds(0, tm), :], ybuf.at[k], sem)
    x2 = x_ref[...] + gt_ref[:, 0:1] * ybuf[0] + gt_ref[:, 1:2] * ybuf[1]
    o_ref[...] = _rms_norm(x2, g_ref[...])


def _combine(x1, ys, pos3, gates_t, g_final, *, tm):
    t, d = x1.shape
    nbytes = 2 * (2 * tm * d * 4 + tm * LANES * 4) + TOP_K * tm * d * 4 + 2 * tm * d * 4
    return pl.pallas_call(
        _combine_body,
        grid=(t // tm,),
        in_specs=[_pos_spec(tm),
                  pl.BlockSpec((tm, d), lambda i: (i, 0)),
                  pl.BlockSpec((tm, TOP_K), lambda i: (i, 0)),
                  pl.BlockSpec((1, d), lambda i: (0, 0)),
                  pl.BlockSpec(memory_space=pl.ANY)],
        out_specs=pl.BlockSpec((tm, d), lambda i: (i, 0)),
        out_shape=jax.ShapeDtypeStruct((t, d), F32),
        scratch_shapes=[pltpu.VMEM((TOP_K, tm, d), F32), pltpu.SemaphoreType.DMA],
        compiler_params=_params(1, nbytes),
        name="combine",
    )(pos3, x1, gates_t, g_final.reshape(1, d), ys)


def _tiles(t):
    return dict(tm_front=min(256, t), tr=min(2048, t), tm_rows=min(256, t), bm=min(256, TOP_K * t))


def _router_params(w_group, b_group, w_router, b_router):
    d = w_group.shape[0]
    gpad = SUBLANES - N_GROUPS
    wr = jnp.concatenate([w_group, jnp.zeros((d, gpad), F32), w_router,
                          jnp.zeros((d, LANES - ROUTER_ROWS), F32)], axis=1)
    br = jnp.concatenate([b_group, jnp.zeros((gpad,), F32), b_router,
                          jnp.zeros((LANES - ROUTER_ROWS,), F32)]).reshape(1, LANES)
    return wr, br


def _moe(x1, hn, logits, w1, w3, w2, layer, tl):
    t, d = x1.shape
    meta, gates, cnt = _route(logits, tr=tl["tr"])

    counts = cnt[:, 0].astype(I32)
    start = jnp.cumsum(counts) - counts
    is_e = meta[0:TOP_K, :, None] == jnp.arange(N_EXPERTS, dtype=I32)
    pos = jnp.sum(jnp.where(is_e, start, 0), axis=-1) + meta[TOP_K:2 * TOP_K]
    tm = tl["tm_rows"]
    pos3 = pos.reshape(TOP_K, t // tm, tm).transpose(1, 0, 2).reshape(t // tm, 1, TOP_K * tm)

    xs = _scatter_rows(hn, pos3, tm=tm)
    tables = _visit_tables(counts, TOP_K * t, tl["bm"])
    ys = _experts(xs, w1, w3, w2, tables, layer=layer, bm=tl["bm"])
    return x1, gates[0:TOP_K].T, pos3, ys


def kernel(x, ln_mix, ln_ffn, ln_final, ev_w_in, ev_conv_w, ev_sgu_ln_g, ev_sgu_ln_b, ev_sgu_w, ev_sgu_b, ev_w_out, od_w_in, od_pool_w, od_pool_scale, od_dw_w, od_dw_b, od_ln_g, od_ln_b, od_w_out, moe_w_group, moe_b_group, moe_w_router, moe_b_router, moe_w1, moe_w3, moe_w2):
    bsz, s, d = x.shape
    assert bsz == 1, "token mixers carry state across row tiles of one sequence"
    depth = ln_mix.shape[0]
    tl = _tiles(s)
    tm = tl["tm_front"]
    assert tm == tl["tm_rows"], "the fused gather reads pos3 in front-kernel tiles"
    x_in = x.reshape(s, d)
    for layer in range(depth):
        i = layer // 2
        wr, br = _router_params(moe_w_group[layer], moe_b_group[layer],
                                moe_w_router[layer], moe_b_router[layer])
        if layer % 2 == 0:
            c = ev_conv_w.shape[2]
            mix_params = [ev_conv_w[i], ev_sgu_ln_g[i].reshape(1, c), ev_sgu_ln_b[i].reshape(1, c),
                          ev_sgu_w[i], ev_sgu_b[i].T]
            mix_scratch = [pltpu.VMEM((tm + EVEN_HALO, c), F32)]
            x1, hn, logits = _layer_front(
                "even_layer", _even_mix, x_in, ln_mix[layer], ev_w_in[i].astype(BF16), mix_params,
                mix_scratch, ev_w_out[i].astype(BF16), ln_ffn[layer], wr, br, tm=tm)
        else:
            c = od_dw_w.shape[2]
            mix_params = [od_pool_w[i], od_pool_scale[i].reshape(1, c), od_dw_w[i],
                          od_dw_b[i].reshape(1, c), od_ln_g[i].reshape(1, c), od_ln_b[i].reshape(1, c)]
            gd = c // len(POOL_WINDOWS)
            mix_scratch = [pltpu.VMEM((tm + POOL_HALO, c), F32),
                           pltpu.VMEM((tm + POOL_HALO, gd), F32), pltpu.VMEM((tm + POOL_HALO, gd), F32),
                           pltpu.VMEM((tm + CONF_HALO, c), F32),
                           pltpu.VMEM((SUBLANES - 1, tm + CONF_HALO, LANES), F32),
                           pltpu.VMEM((tm, c), F32)]
            x1, hn, logits = _layer_front(
                "odd_layer", _odd_mix, x_in, ln_mix[layer], od_w_in[i].astype(BF16), mix_params,
                mix_scratch, od_w_out[i].astype(BF16), ln_ffn[layer], wr, br, tm=tm)
        x_in = _moe(x1, hn, logits, moe_w1, moe_w3, moe_w2, layer, tl)
    x1, gates_t, pos3, ys = x_in
    out = _combine(x1, ys, pos3, gates_t, ln_final, tm=tl["tm_rows"])
    return out.reshape(bsz, s, d)
```

```python
import functools
import math

import jax
import jax.numpy as jnp
from jax import lax
from jax.experimental import pallas as pl
from jax.experimental.pallas import tpu as pltpu

F32 = jnp.float32
BF16 = jnp.bfloat16
I32 = jnp.int32

EPS = 1e-6
CHUNK = 64
SGU_BLOCK = 128
SGU_HEADS = 8
POOL_WINDOWS = (2, 4, 8, 16)
N_GROUPS = 4
EXPERTS_PER_GROUP = 8
N_EXPERTS = N_GROUPS * EXPERTS_PER_GROUP
TOP_K = 2

LANES = 128
SUBLANES = 8
VMEM_CAP_BYTES = 60 * 1024 * 1024

ROUTER_ROWS = SUBLANES + N_EXPERTS
NEG_BIG = -1e30


def _vmem_limit(nbytes):
    return int(min(VMEM_CAP_BYTES, nbytes * 5 // 4 + (6 << 20)))


def _params(n_axes, nbytes):
    return pltpu.CompilerParams(
        dimension_semantics=("arbitrary",) * n_axes,
        vmem_limit_bytes=_vmem_limit(nbytes))


def _const_spec(shape, single_buffer=False):
    mode = dict(pipeline_mode=pl.Buffered(1)) if single_buffer else {}
    return pl.BlockSpec(shape, lambda i: (0,) * len(shape), **mode)


def _sigmoid(x):
    return 1.0 / (1.0 + jnp.exp(-x))


def _gelu_tanh(x):
    c = math.sqrt(2.0 / math.pi)
    return x * (0.5 * (1.0 + jnp.tanh(c * (x + 0.044715 * (x * x * x)))))


def _layer_norm(v, g, b):
    mu = jnp.mean(v, axis=-1, keepdims=True)
    vc = v - mu
    var = jnp.mean(vc * vc, axis=-1, keepdims=True)
    return vc * lax.rsqrt(var + EPS) * g + b


def _rms_norm(x, g):
    ms = jnp.mean(x * x, axis=-1, keepdims=True)
    return x * lax.rsqrt(ms + EPS) * g


def _split_bf16(v):
    head = v.astype(BF16)
    return head, (v - head.astype(F32)).astype(BF16)


IN_PROJ_COLS = 1024


def _in_proj(x_ref, g_ref, w_ref, z_ref):
    h = _rms_norm(x_ref[...], g_ref[...]).astype(BF16)
    for c0 in range(0, w_ref.shape[1], IN_PROJ_COLS):
        cols = slice(c0, c0 + IN_PROJ_COLS)
        z_ref[:, cols] = jnp.dot(h, w_ref[:, cols], preferred_element_type=F32)


def _out_proj_route(x_ref, y_ref, w_ref, g_ref, wh_ref, wl_ref, br_ref, x1_ref, hn_ref, lg_ref):
    x1 = x_ref[...] + jnp.dot(y_ref[...], w_ref[...], preferred_element_type=F32)
    x1_ref[...] = x1
    hn = _rms_norm(x1, g_ref[...])
    hn_ref[...] = hn
    head, rem = _split_bf16(hn)
    lg_ref[...] = (jnp.dot(head, wh_ref[...], preferred_element_type=F32)
                   + jnp.dot(rem, wh_ref[...], preferred_element_type=F32)
                   + jnp.dot(head, wl_ref[...], preferred_element_type=F32)
                   + br_ref[...])


EVEN_HALO = SUBLANES


def _even_mix(z_ref, cw_ref, lg_ref, lb_ref, sw_ref, sb_ref, o_ref, pbuf):
    tm = z_ref.shape[0]
    kw, c = cw_ref.shape
    col = lambda j: z_ref[:, j * c:(j + 1) * c]

    @pl.when(pl.program_id(0) == 0)
    def _():
        pbuf[0:EVEN_HALO, :] = jnp.zeros((EVEN_HALO, c), F32)

    p = col(1) * col(2)
    pbuf[EVEN_HALO:EVEN_HALO + tm, :] = p
    conv = cw_ref[kw - 1:kw, :] * p
    for k in range(kw - 1):
        conv = conv + cw_ref[k:k + 1, :] * pbuf[pl.ds(EVEN_HALO - (kw - 1) + k, tm), :]
    o_ref[:, 0:c] = (col(0) * conv).astype(BF16)
    pbuf[0:EVEN_HALO, :] = pbuf[tm:tm + EVEN_HALO, :]

    u = _gelu_tanh(col(3))
    vn = _layer_norm(_gelu_tanh(col(4)), lg_ref[...], lb_ref[...]).astype(BF16)
    tpos = lax.broadcasted_iota(I32, (SGU_BLOCK, SGU_BLOCK), 0)
    spos = lax.broadcasted_iota(I32, (SGU_BLOCK, SGU_BLOCK), 1)
    mask = (spos // CHUNK) <= (tpos // CHUNK)
    hd = c // SGU_HEADS
    for h in range(SGU_HEADS):
        wh = jnp.where(mask, sw_ref[h], 0.0).astype(BF16)
        bh = sb_ref[:, h:h + 1]
        cols = slice(h * hd, (h + 1) * hd)
        for n in range(tm // SGU_BLOCK):
            rows = slice(n * SGU_BLOCK, (n + 1) * SGU_BLOCK)
            sv = jnp.dot(wh, vn[rows, cols], preferred_element_type=F32) + bh
            o_ref[rows, c + h * hd:c + (h + 1) * hd] = (u[rows, cols] * sv).astype(BF16)


POOL_HALO = 24
CONF_HALO = 32
CONV_ROWS = 128


def _window_sum(xbuf, cols, w, lv_a, lv_b, tm):
    lo, hi = SUBLANES, POOL_HALO + tm
    src, src_cols, n = xbuf, cols, 1
    dst, other = lv_a, lv_b
    while 2 * n < w:
        dst[lo:hi, :] = src[lo:hi, src_cols] + src[pl.ds(lo - n, hi - lo), src_cols]
        src, src_cols, n = dst, slice(None), 2 * n
        dst, other = other, dst
    return src[POOL_HALO:hi, src_cols] + src[pl.ds(POOL_HALO - n, tm), src_cols]


def _odd_mix(z_ref, pw_ref, ps_ref, dw_ref, db_ref, lg_ref, lb_ref, o_ref,
             xbuf, lv_a, lv_b, gbuf, sbuf, dbuf):
    tm = z_ref.shape[0]
    kw, c = dw_ref.shape
    i = pl.program_id(0)

    @pl.when(i == 0)
    def _():
        xbuf[0:POOL_HALO, :] = jnp.zeros((POOL_HALO, c), F32)
        gbuf[0:CONF_HALO, :] = jnp.zeros((CONF_HALO, c), F32)
        lv_a[0:SUBLANES, :] = jnp.zeros((SUBLANES, lv_a.shape[1]), F32)
        lv_b[0:SUBLANES, :] = jnp.zeros((SUBLANES, lv_b.shape[1]), F32)

    x = z_ref[:, 0:c]
    xbuf[POOL_HALO:POOL_HALO + tm, :] = x
    step = i * tm + lax.broadcasted_iota(I32, (tm, 1), 0) + 1
    gd = c // len(POOL_WINDOWS)
    for g, w in enumerate(POOL_WINDOWS):
        cols = slice(g * gd, (g + 1) * gd)
        s = _window_sum(xbuf, cols, w, lv_a, lv_b, tm)
        inv_count = 1.0 / jnp.minimum(step, w).astype(F32)
        d = s * inv_count - x[:, cols]
        y = jnp.dot(d.astype(BF16), pw_ref[g].astype(BF16), preferred_element_type=F32)
        o_ref[:, cols] = (y * ps_ref[:, cols]).astype(BF16)
    xbuf[0:POOL_HALO, :] = xbuf[tm:tm + POOL_HALO, :]

    gbuf[CONF_HALO:CONF_HALO + tm, :] = z_ref[:, c:2 * c] * _sigmoid(z_ref[:, 2 * c:3 * c])
    n_rows = CONF_HALO + tm

    def conv_cols(cb, carry):
        lanes = pl.ds(pl.multiple_of(cb * LANES, LANES), LANES)
        for b in range(1, SUBLANES):
            sbuf[b - 1, SUBLANES:n_rows, :] = gbuf[pl.ds(SUBLANES - b, n_rows - SUBLANES), lanes]
        accs = [jnp.broadcast_to(db_ref[:, lanes], (CONV_ROWS, LANES)) for _ in range(0, tm, CONV_ROWS)]
        for back in range(kw):
            a, b = divmod(back, SUBLANES)
            wk = dw_ref[kw - 1 - back:kw - back, lanes]
            for n, r0 in enumerate(range(0, tm, CONV_ROWS)):
                start = r0 + CONF_HALO - SUBLANES * a
                tap = gbuf[start:start + CONV_ROWS, lanes] if b == 0 else sbuf[b - 1, start:start + CONV_ROWS, :]
                accs[n] = accs[n] + wk * tap
        for n, r0 in enumerate(range(0, tm, CONV_ROWS)):
            dbuf[r0:r0 + CONV_ROWS, lanes] = accs[n]
        return carry

    lax.fori_loop(0, c // LANES, conv_cols, 0)
    gbuf[0:CONF_HALO, :] = gbuf[tm:tm + CONF_HALO, :]
    yn = _layer_norm(dbuf[...], lg_ref[...], lb_ref[...])
    o_ref[:, c:2 * c] = (yn * _sigmoid(yn)).astype(BF16)


def _start_row_gathers(pos_ref, ys_ref, ybuf, sem, slot):
    tm = ybuf.shape[2]
    for r in range(tm):
        for k in range(TOP_K):
            p = pos_ref[0, 0, k * tm + r]
            pltpu.make_async_copy(ys_ref.at[pl.ds(p, 1), :], ybuf.at[slot, k, pl.ds(r, 1), :],
                                  sem.at[slot]).start(priority=k)


def _wait_row_gathers(ys_ref, ybuf, sem, slot):
    tm = ybuf.shape[2]
    for k in range(TOP_K):
        _wait_rows(ys_ref.at[pl.ds(0, tm), :], ybuf.at[slot, k], sem.at[slot])


def _gathered_input(pos_cur, pos_next, x1_ref, gt_ref, ys_ref, x_ref, ybuf, sem):
    i = pl.program_id(0)
    slot = i % 2

    @pl.when(i == 0)
    def _():
        _start_row_gathers(pos_cur, ys_ref, ybuf, sem, 0)

    _wait_row_gathers(ys_ref, ybuf, sem, slot)
    x_ref[...] = x1_ref[...] + gt_ref[:, 0:1] * ybuf[slot, 0] + gt_ref[:, 1:2] * ybuf[slot, 1]
    _start_row_gathers(pos_next, ys_ref, ybuf, sem, 1 - slot)


def _front_body(*refs, mix, n_mix_params, gathered):
    n_head = 5 if gathered else 1
    head, refs = refs[:n_head], refs[n_head:]
    gm_ref, win_ref = refs[0:2]
    mix_params = refs[2:2 + n_mix_params]
    wout_ref, gf_ref, wh_ref, wl_ref, br_ref, x1_ref, hn_ref, lgt_ref, z_ref, y_ref = (
        refs[2 + n_mix_params:12 + n_mix_params])
    scratch = refs[12 + n_mix_params:]
    if gathered:
        mix_scratch, (x_ref, ybuf, sem) = scratch[:-3], scratch[-3:]
        _gathered_input(*head, x_ref, ybuf, sem)
    else:
        mix_scratch, x_ref = scratch, head[0]
    _in_proj(x_ref, gm_ref, win_ref, z_ref)
    mix(z_ref, *mix_params, y_ref, *mix_scratch)
    _out_proj_route(x_ref, y_ref, wout_ref, gf_ref, wh_ref, wl_ref, br_ref, x1_ref, hn_ref, lgt_ref)
    if gathered:
        @pl.when(pl.program_id(0) == pl.num_programs(0) - 1)
        def _():
            _wait_row_gathers(head[4], ybuf, sem, 1 - pl.program_id(0) % 2)


def _layer_front(name, mix, x_in, g_mix, w_in, mix_params, mix_scratch, w_out, g_ffn, wr, br, *, tm):
    gathered = isinstance(x_in, tuple)
    t, d = (x_in[0] if gathered else x_in).shape
    nt = t // tm
    n = w_in.shape[1]
    c2 = w_out.shape[0]
    wh, wl = _split_bf16(wr)
    row = lambda width: pl.BlockSpec((tm, width), lambda i: (i, 0))
    tail = [g_ffn.reshape(1, d), wh, wl, br]
    scratch = [pltpu.VMEM((tm, n), F32), pltpu.VMEM((tm, c2), BF16)] + mix_scratch
    if gathered:
        x1, gates_t, pos3, ys = x_in
        head = [pos3, pos3, x1, gates_t, ys]
        head_specs = [_pos_spec(tm),
                      pl.BlockSpec((1, 1, TOP_K * tm), lambda i: (jnp.minimum(i + 1, nt - 1), 0, 0),
                                   memory_space=pltpu.SMEM),
                      row(d), pl.BlockSpec((tm, TOP_K), lambda i: (i, 0)),
                      pl.BlockSpec(memory_space=pl.ANY)]
        scratch += [pltpu.VMEM((tm, d), F32), pltpu.VMEM((2, TOP_K, tm, d), F32),
                    pltpu.SemaphoreType.DMA((2,))]
    else:
        head, head_specs = [x_in], [row(d)]
    nbytes = (d * n * 2 + c2 * d * 2
              + 2 * 3 * tm * d * 4 + 2 * tm * LANES * 4
              + tm * n * 4 + tm * c2 * 2
              + sum(math.prod(s.shape) * 4 for s in mix_scratch)
              + 2 * sum(p.size * p.dtype.itemsize for p in mix_params + tail[1:])
              + (2 * TOP_K + 1) * tm * d * 4 * gathered
              + 6 * tm * d * 4)
    return pl.pallas_call(
        functools.partial(_front_body, mix=mix, n_mix_params=len(mix_params), gathered=gathered),
        grid=(nt,),
        in_specs=(head_specs + [_const_spec((1, d)), _const_spec(w_in.shape, single_buffer=True)]
                  + [_const_spec(p.shape) for p in mix_params]
                  + [_const_spec(w_out.shape, single_buffer=True), _const_spec((1, d)),
                     _const_spec(wh.shape), _const_spec(wl.shape), _const_spec(br.shape)]),
        out_specs=[row(d), row(d), row(LANES)],
        out_shape=[jax.ShapeDtypeStruct((t, d), F32),
                   jax.ShapeDtypeStruct((t, d), F32),
                   jax.ShapeDtypeStruct((t, LANES), F32)],
        scratch_shapes=scratch,
        compiler_params=_params(1, nbytes),
        name=name,
    )(*head, g_mix.reshape(1, d), w_in, *mix_params, w_out, *tail)


RANK_BLOCK = 256


def _route_block(logits, run, before):
    n = logits.shape[1]
    row8 = lax.broadcasted_iota(I32, (SUBLANES, n), 0)
    row = lax.broadcasted_iota(I32, (N_EXPERTS, n), 0)

    gl = jnp.where(row8 < N_GROUPS, logits[0:SUBLANES], NEG_BIG)
    gmax = jnp.max(gl, axis=0, keepdims=True)
    gidx = jnp.min(jnp.where(gl == gmax, row8, N_EXPERTS), axis=0, keepdims=True)
    p_group = 1.0 / jnp.sum(jnp.exp(gl - gmax), axis=0, keepdims=True)

    el = jnp.where(row // EXPERTS_PER_GROUP == gidx, logits[SUBLANES:ROUTER_ROWS], NEG_BIG)
    m0 = jnp.max(el, axis=0, keepdims=True)
    e0 = jnp.min(jnp.where(el == m0, row, N_EXPERTS), axis=0, keepdims=True)
    el1 = jnp.where(row == e0, NEG_BIG, el)
    m1 = jnp.max(el1, axis=0, keepdims=True)
    e1 = jnp.min(jnp.where(el1 == m1, row, N_EXPERTS), axis=0, keepdims=True)
    ratio = jnp.exp(m1 - m0)
    gate0 = p_group / (1.0 + ratio)
    gate1 = p_group * ratio / (1.0 + ratio)

    sel0 = row == e0
    sel1 = row == e1
    pre0 = jnp.dot(sel0.astype(BF16), before, preferred_element_type=F32)
    pre1 = jnp.dot(sel1.astype(BF16), before, preferred_element_type=F32)
    cnt0 = jnp.sum(sel0.astype(F32), axis=1, keepdims=True)
    cnt1 = jnp.sum(sel1.astype(F32), axis=1, keepdims=True)
    rank0 = jnp.sum(jnp.where(sel0, pre0 + run, 0.0), axis=0, keepdims=True)
    rank1 = jnp.sum(jnp.where(sel1, pre1 + (run + cnt0), 0.0), axis=0, keepdims=True)

    meta = jnp.where(row8 == 0, e0, jnp.where(row8 == 1, e1, jnp.where(
        row8 == 2, rank0.astype(I32), jnp.where(row8 == 3, rank1.astype(I32), 0))))
    gates = jnp.where(row8 == 0, gate0, jnp.where(row8 == 1, gate1, 0.0))
    return meta, gates, run + (cnt0 + cnt1)


def _route_body(lg_ref, meta_ref, gate_ref, cnt_ref, run_ref):
    tr = lg_ref.shape[0]
    nb = min(RANK_BLOCK, tr)

    @pl.when(pl.program_id(0) == 0)
    def _():
        run_ref[...] = jnp.zeros(run_ref.shape, F32)

    src = lax.broadcasted_iota(I32, (nb, nb), 0)
    dst = lax.broadcasted_iota(I32, (nb, nb), 1)
    before = (src < dst).astype(BF16)
    run = run_ref[:, 0:1]
    for b in range(tr // nb):
        cols = slice(b * nb, (b + 1) * nb)
        logits_t = lg_ref[cols, :].T[0:ROUTER_ROWS]
        meta, gates, run = _route_block(logits_t, run, before)
        meta_ref[:, cols] = meta
        gate_ref[:, cols] = gates
    run_ref[...] = jnp.broadcast_to(run, run_ref.shape)
    cnt_ref[...] = run_ref[...]


def _route(logits, *, tr):
    t = logits.shape[0]
    return pl.pallas_call(
        _route_body,
        grid=(t // tr,),
        in_specs=[pl.BlockSpec((tr, LANES), lambda i: (i, 0))],
        out_specs=[pl.BlockSpec((SUBLANES, tr), lambda i: (0, i)),
                   pl.BlockSpec((SUBLANES, tr), lambda i: (0, i)),
                   pl.BlockSpec((N_EXPERTS, LANES), lambda i: (0, 0))],
        out_shape=[jax.ShapeDtypeStruct((SUBLANES, t), I32),
                   jax.ShapeDtypeStruct((SUBLANES, t), F32),
                   jax.ShapeDtypeStruct((N_EXPERTS, LANES), F32)],
        scratch_shapes=[pltpu.VMEM((N_EXPERTS, LANES), F32)],
        compiler_params=_params(1, 64 * tr * 4 * 4),
        name="route",
    )(logits)


def _pos_spec(tm):
    return pl.BlockSpec((1, 1, TOP_K * tm), lambda i: (i, 0, 0), memory_space=pltpu.SMEM)


def _wait_rows(src_ref, dst_ref, sem):
    pltpu.make_async_copy(src_ref, dst_ref, sem).wait()


def _scatter_body(pos_ref, h_ref, xs_ref, sem):
    tm = h_ref.shape[0]

    for r in range(tm):
        for k in range(TOP_K):
            p = pos_ref[0, 0, k * tm + r]
            pltpu.make_async_copy(h_ref.at[pl.ds(r, 1), :], xs_ref.at[pl.ds(p, 1), :],
                                  sem).start(priority=k)
    for k in range(TOP_K):
        _wait_rows(h_ref, xs_ref.at[pl.ds(0, tm), :], sem)


def _scatter_rows(hn, pos3, *, tm):
    t, d = hn.shape
    return pl.pallas_call(
        _scatter_body,
        grid=(t // tm,),
        in_specs=[_pos_spec(tm), pl.BlockSpec((tm, d), lambda i: (i, 0))],
        out_specs=pl.BlockSpec(memory_space=pl.ANY),
        out_shape=jax.ShapeDtypeStruct((TOP_K * t, d), hn.dtype),
        scratch_shapes=[pltpu.SemaphoreType.DMA],
        compiler_params=_params(1, 2 * tm * d * 4),
        name="scatter_rows",
    )(pos3, hn)


def _expert_body(vblk, vexp, vfirst, vnew, vvalid, vslot, vnext, vhalf, estart,
                 x_ref, w1_hbm, w3_hbm, w2_hbm, o_ref, wf1, wf3, wf2, w1b, w3b, w2b, sem, *, layer):
    bm = x_ref.shape[0]
    v = pl.program_id(0)
    weights = ((w1_hbm, wf1), (w3_hbm, wf3), (w2_hbm, wf2))

    def weight_copies(e, slot):
        return [pltpu.make_async_copy(w_hbm.at[layer, e], wf.at[slot], sem.at[slot, j])
                for j, (w_hbm, wf) in enumerate(weights)]

    @pl.when(v == 0)
    def _():
        for c in weight_copies(vexp[0], 0):
            c.start()

    @pl.when(vvalid[v] == 1)
    def _():
        @pl.when(vnew[v] == 1)
        def _():
            slot = vslot[v]
            for c in weight_copies(vexp[v], slot):
                c.wait()

            @pl.when(vnext[v] >= 0)
            def _():
                for c in weight_copies(vnext[v], 1 - slot):
                    c.start()

            w1b[...] = wf1[slot].astype(BF16)
            w3b[...] = wf3[slot].astype(BF16)
            w2b[...] = wf2[slot].astype(BF16)

        def visit(r0, n, r0_other, n_other):
            rows = slice(r0, r0 + n)
            xb = x_ref[rows, :].astype(BF16)
            h1 = jnp.dot(xb, w1b[...], preferred_element_type=F32)
            h3 = jnp.dot(xb, w3b[...], preferred_element_type=F32)
            hid = (h1 * _sigmoid(h1)) * h3
            y = jnp.dot(hid.astype(BF16), w2b[...], preferred_element_type=F32)

            @pl.when(vfirst[v] == 1)
            def _():
                o_ref[rows, :] = y
                if n_other:
                    o_ref[r0_other:r0_other + n_other, :] = jnp.zeros((n_other, o_ref.shape[1]), F32)

            @pl.when(vfirst[v] == 0)
            def _():
                e = vexp[v]
                r = vblk[v] * bm + r0 + lax.broadcasted_iota(I32, (n, 1), 0)
                mine = (r >= estart[e]) & (r < estart[e + 1])
                o_ref[rows, :] = jnp.where(mine, y, o_ref[rows, :])

        hb = bm // 2

        @pl.when(vhalf[v] == 0)
        def _():
            visit(0, bm, 0, 0)

        @pl.when(vhalf[v] == 1)
        def _():
            visit(0, hb, hb, bm - hb)

        @pl.when(vhalf[v] == 2)
        def _():
            visit(hb, bm - hb, 0, hb)


def _experts(xs, w1, w3, w2, tables, *, layer, bm):
    n, d = xs.shape
    f = w1.shape[3]
    n_visits = tables[0].shape[0]
    nbytes = 2 * (bm * d * 4 * 2 + 3 * d * f * 4) + 3 * d * f * 2 + 6 * bm * d * 4
    hbm = pl.BlockSpec(memory_space=pl.ANY)
    grid_spec = pltpu.PrefetchScalarGridSpec(
        num_scalar_prefetch=len(tables),
        grid=(n_visits,),
        in_specs=[pl.BlockSpec((bm, d), lambda v, blk, *_: (blk[v], 0)), hbm, hbm, hbm],
        out_specs=pl.BlockSpec((bm, d), lambda v, blk, *_: (blk[v], 0)),
        scratch_shapes=[pltpu.VMEM((2, d, f), F32), pltpu.VMEM((2, d, f), F32), pltpu.VMEM((2, f, d), F32),
                        pltpu.VMEM((d, f), BF16), pltpu.VMEM((d, f), BF16), pltpu.VMEM((f, d), BF16),
                        pltpu.SemaphoreType.DMA((2, 3))])
    return pl.pallas_call(
        functools.partial(_expert_body, layer=layer),
        grid_spec=grid_spec,
        out_shape=jax.ShapeDtypeStruct((n, d), F32),
        compiler_params=_params(1, nbytes),
        name="experts",
    )(*tables, xs, w1, w3, w2)


def _visit_tables(counts, n_slots, bm):
    n_blocks = n_slots // bm
    n_visits = n_blocks + N_EXPERTS - 1
    end = jnp.cumsum(counts)
    start = end - counts
    first_blk = start // bm
    last_blk = jnp.maximum(end - 1, 0) // bm
    per_e = jnp.where(counts > 0, last_blk - first_blk + 1, 0)
    cum = jnp.cumsum(per_e)
    total = cum[-1]
    v = jnp.minimum(jnp.arange(n_visits, dtype=I32), total - 1)
    e = jnp.sum((cum[None, :] <= v[:, None]).astype(I32), axis=1)
    blk = (first_blk[e] + (v - (cum[e] - per_e[e]))).astype(I32)
    valid = (jnp.arange(n_visits) < total).astype(I32)
    prev_blk = jnp.concatenate([jnp.full((1,), -1, I32), blk[:-1]])
    prev_e = jnp.concatenate([jnp.full((1,), -1, I32), e[:-1]])
    first = (blk != prev_blk).astype(I32)
    new = (e != prev_e).astype(I32)
    ordinal = jnp.cumsum(new) - 1
    slot = (ordinal % 2).astype(I32)
    is_next = (new[None, :] == 1) & (ordinal[None, :] == ordinal[:, None] + 1)
    nxt = jnp.where(jnp.any(is_next, axis=1), jnp.sum(jnp.where(is_next, e[None, :], 0), axis=1), -1)
    lo = jnp.maximum(start[e], blk * bm) - blk * bm
    hi = jnp.minimum(end[e], (blk + 1) * bm) - blk * bm
    half = jnp.where(hi <= bm // 2, 1, jnp.where(lo >= bm // 2, 2, 0)).astype(I32)
    estart = jnp.concatenate([start, end[-1:]]).astype(I32)
    return blk, e, first, new, valid, slot, nxt.astype(I32), half, estart


def _combine_body(pos_ref, x_ref, gt_ref, g_ref, ys_ref, o_ref, ybuf, sem):
    tm = x_ref.shape[0]
    for r in range(tm):
        for k in range(TOP_K):
            p = pos_ref[0, 0, k * tm + r]
            pltpu.make_async_copy(ys_ref.at[pl.ds(p, 1), :], ybuf.at[k, pl.ds(r, 1), :],
                                  sem).start(priority=k)
    for k in range(TOP_K):
        _wait_rows(ys_ref.at[pl.ds(0, tm), :], ybuf.at[k], sem)
    x2 = x_ref[...] + gt_ref[:, 0:1] * ybuf[0] + gt_ref[:, 1:2] * ybuf[1]
    o_ref[...] = _rms_norm(x2, g_ref[...])


def _combine(x1, ys, pos3, gates_t, g_final, *, tm):
    t, d = x1.shape
    nbytes = 2 * (2 * tm * d * 4 + tm * LANES * 4) + TOP_K * tm * d * 4 + 2 * tm * d * 4
    return pl.pallas_call(
        _combine_body,
        grid=(t // tm,),
        in_specs=[_pos_spec(tm),
                  pl.BlockSpec((tm, d), lambda i: (i, 0)),
                  pl.BlockSpec((tm, TOP_K), lambda i: (i, 0)),
                  pl.BlockSpec((1, d), lambda i: (0, 0)),
                  pl.BlockSpec(memory_space=pl.ANY)],
        out_specs=pl.BlockSpec((tm, d), lambda i: (i, 0)),
        out_shape=jax.ShapeDtypeStruct((t, d), F32),
        scratch_shapes=[pltpu.VMEM((TOP_K, tm, d), F32), pltpu.SemaphoreType.DMA],
        compiler_params=_params(1, nbytes),
        name="combine",
    )(pos3, x1, gates_t, g_final.reshape(1, d), ys)


def _tiles(t):
    return dict(tm_front=min(256, t), tr=min(2048, t), tm_rows=min(256, t), bm=min(256, TOP_K * t))


def _router_params(w_group, b_group, w_router, b_router):
    d = w_group.shape[0]
    gpad = SUBLANES - N_GROUPS
    wr = jnp.concatenate([w_group, jnp.zeros((d, gpad), F32), w_router,
                          jnp.zeros((d, LANES - ROUTER_ROWS), F32)], axis=1)
    br = jnp.concatenate([b_group, jnp.zeros((gpad,), F32), b_router,
                          jnp.zeros((LANES - ROUTER_ROWS,), F32)]).reshape(1, LANES)
    return wr, br


def _moe(x1, hn, logits, w1, w3, w2, layer, tl):
    t, d = x1.shape
    meta, gates, cnt = _route(logits, tr=tl["tr"])

    counts = cnt[:, 0].astype(I32)
    start = jnp.cumsum(counts) - counts
    is_e = meta[0:TOP_K, :, None] == jnp.arange(N_EXPERTS, dtype=I32)
    pos = jnp.sum(jnp.where(is_e, start, 0), axis=-1) + meta[TOP_K:2 * TOP_K]
    tm = tl["tm_rows"]
    pos3 = pos.reshape(TOP_K, t // tm, tm).transpose(1, 0, 2).reshape(t // tm, 1, TOP_K * tm)

    xs = _scatter_rows(hn, pos3, tm=tm)
    tables = _visit_tables(counts, TOP_K * t, tl["bm"])
    ys = _experts(xs, w1, w3, w2, tables, layer=layer, bm=tl["bm"])
    return x1, gates[0:TOP_K].T, pos3, ys


def kernel(x, ln_mix, ln_ffn, ln_final, ev_w_in, ev_conv_w, ev_sgu_ln_g, ev_sgu_ln_b, ev_sgu_w, ev_sgu_b, ev_w_out, od_w_in, od_pool_w, od_pool_scale, od_dw_w, od_dw_b, od_ln_g, od_ln_b, od_w_out, moe_w_group, moe_b_group, moe_w_router, moe_b_router, moe_w1, moe_w3, moe_w2):
    bsz, s, d = x.shape
    assert bsz == 1, "token mixers carry state across row tiles of one sequence"
    depth = ln_mix.shape[0]
    tl = _tiles(s)
    tm = tl["tm_front"]
    assert tm == tl["tm_rows"], "the fused gather reads pos3 in front-kernel tiles"
    x_in = x.reshape(s, d)
    for layer in range(depth):
        i = layer // 2
        wr, br = _router_params(moe_w_group[layer], moe_b_group[layer],
                                moe_w_router[layer], moe_b_router[layer])
        if layer % 2 == 0:
            c = ev_conv_w.shape[2]
            mix_params = [ev_conv_w[i], ev_sgu_ln_g[i].reshape(1, c), ev_sgu_ln_b[i].reshape(1, c),
                          ev_sgu_w[i], ev_sgu_b[i].T]
            mix_scratch = [pltpu.VMEM((tm + EVEN_HALO, c), F32)]
            x1, hn, logits = _layer_front(
                "even_layer", _even_mix, x_in, ln_mix[layer], ev_w_in[i].astype(BF16), mix_params,
                mix_scratch, ev_w_out[i].astype(BF16), ln_ffn[layer], wr, br, tm=tm)
        else:
            c = od_dw_w.shape[2]
            mix_params = [od_pool_w[i], od_pool_scale[i].reshape(1, c), od_dw_w[i],
                          od_dw_b[i].reshape(1, c), od_ln_g[i].reshape(1, c), od_ln_b[i].reshape(1, c)]
            gd = c // len(POOL_WINDOWS)
            mix_scratch = [pltpu.VMEM((tm + POOL_HALO, c), F32),
                           pltpu.VMEM((tm + POOL_HALO, gd), F32), pltpu.VMEM((tm + POOL_HALO, gd), F32),
                           pltpu.VMEM((tm + CONF_HALO, c), F32),
                           pltpu.VMEM((SUBLANES - 1, tm + CONF_HALO, LANES), F32),
                           pltpu.VMEM((tm, c), F32)]
            x1, hn, logits = _layer_front(
                "odd_layer", _odd_mix, x_in, ln_mix[layer], od_w_in[i].astype(BF16), mix_params,
                mix_scratch, od_w_out[i].astype(BF16), ln_ffn[layer], wr, br, tm=tm)
        x_in = _moe(x1, hn, logits, moe_w1, moe_w3, moe_w2, layer, tl)
    x1, gates_t, pos3, ys = x_in
    out = _combine(x1, ys, pos3, gates_t, ln_final, tm=tl["tm_rows"])
    return out.reshape(bsz, s, d)
```

```python
import functools
import math

import jax
import jax.numpy as jnp
from jax import lax
from jax.experimental import pallas as pl
from jax.experimental.pallas import tpu as pltpu

F32 = jnp.float32
BF16 = jnp.bfloat16
I32 = jnp.int32

EPS = 1e-6
CHUNK = 64
SGU_BLOCK = 128
SGU_HEADS = 8
POOL_WINDOWS = (2, 4, 8, 16)
N_GROUPS = 4
EXPERTS_PER_GROUP = 8
N_EXPERTS = N_GROUPS * EXPERTS_PER_GROUP
TOP_K = 2

LANES = 128
SUBLANES = 8
VMEM_CAP_BYTES = 60 * 1024 * 1024

ROUTER_ROWS = SUBLANES + N_EXPERTS
NEG_BIG = -1e30


def _vmem_limit(nbytes):
    return int(min(VMEM_CAP_BYTES, nbytes * 5 // 4 + (6 << 20)))


def _params(n_axes, nbytes):
    return pltpu.CompilerParams(
        dimension_semantics=("arbitrary",) * n_axes,
        vmem_limit_bytes=_vmem_limit(nbytes))


def _const_spec(shape, single_buffer=False):
    mode = dict(pipeline_mode=pl.Buffered(1)) if single_buffer else {}
    return pl.BlockSpec(shape, lambda i: (0,) * len(shape), **mode)


def _sigmoid(x):
    return 1.0 / (1.0 + jnp.exp(-x))


def _gelu_tanh(x):
    c = math.sqrt(2.0 / math.pi)
    return x * (0.5 * (1.0 + jnp.tanh(c * (x + 0.044715 * (x * x * x)))))


def _layer_norm(v, g, b):
    mu = jnp.mean(v, axis=-1, keepdims=True)
    vc = v - mu
    var = jnp.mean(vc * vc, axis=-1, keepdims=True)
    return vc * lax.rsqrt(var + EPS) * g + b


def _rms_norm(x, g):
    ms = jnp.mean(x * x, axis=-1, keepdims=True)
    return x * lax.rsqrt(ms + EPS) * g


def _split_bf16(v):
    head = v.astype(BF16)
    return head, (v - head.astype(F32)).astype(BF16)


IN_PROJ_COLS = 1024


def _in_proj(x_ref, g_ref, w_ref, z_ref):
    h = _rms_norm(x_ref[...], g_ref[...]).astype(BF16)
    for c0 in range(0, w_ref.shape[1], IN_PROJ_COLS):
        cols = slice(c0, c0 + IN_PROJ_COLS)
        z_ref[:, cols] = jnp.dot(h, w_ref[:, cols], preferred_element_type=F32)


def _out_proj_route(x_ref, y_ref, w_ref, g_ref, wh_ref, wl_ref, br_ref, x1_ref, hn_ref, lg_ref):
    x1 = x_ref[...] + jnp.dot(y_ref[...], w_ref[...], preferred_element_type=F32)
    x1_ref[...] = x1
    hn = _rms_norm(x1, g_ref[...])
    hn_ref[...] = hn
    head, rem = _split_bf16(hn)
    lg_ref[...] = (jnp.dot(head, wh_ref[...], preferred_element_type=F32)
                   + jnp.dot(rem, wh_ref[...], preferred_element_type=F32)
                   + jnp.dot(head, wl_ref[...], preferred_element_type=F32)
                   + br_ref[...])


EVEN_HALO = SUBLANES


def _even_mix(z_ref, cw_ref, lg_ref, lb_ref, sw_ref, sb_ref, o_ref, pbuf):
    tm = z_ref.shape[0]
    kw, c = cw_ref.shape
    col = lambda j: z_ref[:, j * c:(j + 1) * c]

    @pl.when(pl.program_id(0) == 0)
    def _():
        pbuf[0:EVEN_HALO, :] = jnp.zeros((EVEN_HALO, c), F32)

    p = col(1) * col(2)
    pbuf[EVEN_HALO:EVEN_HALO + tm, :] = p
    conv = cw_ref[kw - 1:kw, :] * p
    for k in range(kw - 1):
        conv = conv + cw_ref[k:k + 1, :] * pbuf[pl.ds(EVEN_HALO - (kw - 1) + k, tm), :]
    o_ref[:, 0:c] = (col(0) * conv).astype(BF16)
    pbuf[0:EVEN_HALO, :] = pbuf[tm:tm + EVEN_HALO, :]

    u = _gelu_tanh(col(3))
    vn = _layer_norm(_gelu_tanh(col(4)), lg_ref[...], lb_ref[...]).astype(BF16)
    tpos = lax.broadcasted_iota(I32, (SGU_BLOCK, SGU_BLOCK), 0)
    spos = lax.broadcasted_iota(I32, (SGU_BLOCK, SGU_BLOCK), 1)
    mask = (spos // CHUNK) <= (tpos // CHUNK)
    hd = c // SGU_HEADS
    for h in range(SGU_HEADS):
        wh = jnp.where(mask, sw_ref[h], 0.0).astype(BF16)
        bh = sb_ref[:, h:h + 1]
        cols = slice(h * hd, (h + 1) * hd)
        for n in range(tm // SGU_BLOCK):
            rows = slice(n * SGU_BLOCK, (n + 1) * SGU_BLOCK)
            sv = jnp.dot(wh, vn[rows, cols], preferred_element_type=F32) + bh
            o_ref[rows, c + h * hd:c + (h + 1) * hd] = (u[rows, cols] * sv).astype(BF16)


POOL_HALO = 24
CONF_HALO = 32
CONV_ROWS = 128


def _window_sum(xbuf, cols, w, lv_a, lv_b, tm):
    lo, hi = SUBLANES, POOL_HALO + tm
    src, src_cols, n = xbuf, cols, 1
    dst, other = lv_a, lv_b
    while 2 * n < w:
        dst[lo:hi, :] = src[lo:hi, src_cols] + src[pl.ds(lo - n, hi - lo), src_cols]
        src, src_cols, n = dst, slice(None), 2 * n
        dst, other = other, dst
    return src[POOL_HALO:hi, src_cols] + src[pl.ds(POOL_HALO - n, tm), src_cols]


def _odd_mix(z_ref, pw_ref, ps_ref, dw_ref, db_ref, lg_ref, lb_ref, o_ref,
             xbuf, lv_a, lv_b, gbuf, sbuf, dbuf):
    tm = z_ref.shape[0]
    kw, c = dw_ref.shape
    i = pl.program_id(0)

    @pl.when(i == 0)
    def _():
        xbuf[0:POOL_HALO, :] = jnp.zeros((POOL_HALO, c), F32)
        gbuf[0:CONF_HALO, :] = jnp.zeros((CONF_HALO, c), F32)
        lv_a[0:SUBLANES, :] = jnp.zeros((SUBLANES, lv_a.shape[1]), F32)
        lv_b[0:SUBLANES, :] = jnp.zeros((SUBLANES, lv_b.shape[1]), F32)

    x = z_ref[:, 0:c]
    xbuf[POOL_HALO:POOL_HALO + tm, :] = x
    step = i * tm + lax.broadcasted_iota(I32, (tm, 1), 0) + 1
    gd = c // len(POOL_WINDOWS)
    for g, w in enumerate(POOL_WINDOWS):
        cols = slice(g * gd, (g + 1) * gd)
        s = _window_sum(xbuf, cols, w, lv_a, lv_b, tm)
        inv_count = 1.0 / jnp.minimum(step, w).astype(F32)
        d = s * inv_count - x[:, cols]
        y = jnp.dot(d.astype(BF16), pw_ref[g].astype(BF16), preferred_element_type=F32)
        o_ref[:, cols] = (y * ps_ref[:, cols]).astype(BF16)
    xbuf[0:POOL_HALO, :] = xbuf[tm:tm + POOL_HALO, :]

    gbuf[CONF_HALO:CONF_HALO + tm, :] = z_ref[:, c:2 * c] * _sigmoid(z_ref[:, 2 * c:3 * c])
    n_rows = CONF_HALO + tm

    def conv_cols(cb, carry):
        lanes = pl.ds(pl.multiple_of(cb * LANES, LANES), LANES)
        for b in range(1, SUBLANES):
            sbuf[b - 1, SUBLANES:n_rows, :] = gbuf[pl.ds(SUBLANES - b, n_rows - SUBLANES), lanes]
        accs = [jnp.broadcast_to(db_ref[:, lanes], (CONV_ROWS, LANES)) for _ in range(0, tm, CONV_ROWS)]
        for back in range(kw):
            a, b = divmod(back, SUBLANES)
            wk = dw_ref[kw - 1 - back:kw - back, lanes]
            for n, r0 in enumerate(range(0, tm, CONV_ROWS)):
                start = r0 + CONF_HALO - SUBLANES * a
                tap = gbuf[start:start + CONV_ROWS, lanes] if b == 0 else sbuf[b - 1, start:start + CONV_ROWS, :]
                accs[n] = accs[n] + wk * tap
        for n, r0 in enumerate(range(0, tm, CONV_ROWS)):
            dbuf[r0:r0 + CONV_ROWS, lanes] = accs[n]
        return carry

    lax.fori_loop(0, c // LANES, conv_cols, 0)
    gbuf[0:CONF_HALO, :] = gbuf[tm:tm + CONF_HALO, :]
    yn = _layer_norm(dbuf[...], lg_ref[...], lb_ref[...])
    o_ref[:, c:2 * c] = (yn * _sigmoid(yn)).astype(BF16)


def _start_row_gathers(pos_ref, ys_ref, ybuf, sem, slot):
    tm = ybuf.shape[2]
    for r in range(tm):
        for k in range(TOP_K):
            p = pos_ref[0, 0, k * tm + r]
            pltpu.make_async_copy(ys_ref.at[pl.ds(p, 1), :], ybuf.at[slot, k, pl.ds(r, 1), :],
                                  sem.at[slot]).start(priority=k)


def _wait_row_gathers(ys_ref, ybuf, sem, slot):
    tm = ybuf.shape[2]
    for k in range(TOP_K):
        _wait_rows(ys_ref.at[pl.ds(0, tm), :], ybuf.at[slot, k], sem.at[slot])


def _gathered_input(pos_cur, pos_next, x1_ref, gt_ref, ys_ref, x_ref, ybuf, sem):
    i = pl.program_id(0)
    slot = i % 2

    @pl.when(i == 0)
    def _():
        _start_row_gathers(pos_cur, ys_ref, ybuf, sem, 0)

    _wait_row_gathers(ys_ref, ybuf, sem, slot)
    x_ref[...] = x1_ref[...] + gt_ref[:, 0:1] * ybuf[slot, 0] + gt_ref[:, 1:2] * ybuf[slot, 1]
    _start_row_gathers(pos_next, ys_ref, ybuf, sem, 1 - slot)


def _front_body(*refs, mix, n_mix_params, gathered):
    n_head = 5 if gathered else 1
    head, refs = refs[:n_head], refs[n_head:]
    gm_ref, win_ref = refs[0:2]
    mix_params = refs[2:2 + n_mix_params]
    wout_ref, gf_ref, wh_ref, wl_ref, br_ref, x1_ref, hn_ref, lgt_ref, z_ref, y_ref = (
        refs[2 + n_mix_params:12 + n_mix_params])
    scratch = refs[12 + n_mix_params:]
    if gathered:
        mix_scratch, (x_ref, ybuf, sem) = scratch[:-3], scratch[-3:]
        _gathered_input(*head, x_ref, ybuf, sem)
    else:
        mix_scratch, x_ref = scratch, head[0]
    _in_proj(x_ref, gm_ref, win_ref, z_ref)
    mix(z_ref, *mix_params, y_ref, *mix_scratch)
    _out_proj_route(x_ref, y_ref, wout_ref, gf_ref, wh_ref, wl_ref, br_ref, x1_ref, hn_ref, lgt_ref)
    if gathered:
        @pl.when(pl.program_id(0) == pl.num_programs(0) - 1)
        def _():
            _wait_row_gathers(head[4], ybuf, sem, 1 - pl.program_id(0) % 2)


def _layer_front(name, mix, x_in, g_mix, w_in, mix_params, mix_scratch, w_out, g_ffn, wr, br, *, tm):
    gathered = isinstance(x_in, tuple)
    t, d = (x_in[0] if gathered else x_in).shape
    nt = t // tm
    n = w_in.shape[1]
    c2 = w_out.shape[0]
    wh, wl = _split_bf16(wr)
    row = lambda width: pl.BlockSpec((tm, width), lambda i: (i, 0))
    tail = [g_ffn.reshape(1, d), wh, wl, br]
    scratch = [pltpu.VMEM((tm, n), F32), pltpu.VMEM((tm, c2), BF16)] + mix_scratch
    if gathered:
        x1, gates_t, pos3, ys = x_in
        head = [pos3, pos3, x1, gates_t, ys]
        head_specs = [_pos_spec(tm),
                      pl.BlockSpec((1, 1, TOP_K * tm), lambda i: (jnp.minimum(i + 1, nt - 1), 0, 0),
                                   memory_space=pltpu.SMEM),
                      row(d), pl.BlockSpec((tm, TOP_K), lambda i: (i, 0)),
                      pl.BlockSpec(memory_space=pl.ANY)]
        scratch += [pltpu.VMEM((tm, d), F32), pltpu.VMEM((2, TOP_K, tm, d), F32),
                    pltpu.SemaphoreType.DMA((2,))]
    else:
        head, head_specs = [x_in], [row(d)]
    nbytes = (d * n * 2 + c2 * d * 2
              + 2 * 3 * tm * d * 4 + 2 * tm * LANES * 4
              + tm * n * 4 + tm * c2 * 2
              + sum(math.prod(s.shape) * 4 for s in mix_scratch)
              + 2 * sum(p.size * p.dtype.itemsize for p in mix_params + tail[1:])
              + (2 * TOP_K + 1) * tm * d * 4 * gathered
              + 6 * tm * d * 4)
    return pl.pallas_call(
        functools.partial(_front_body, mix=mix, n_mix_params=len(mix_params), gathered=gathered),
        grid=(nt,),
        in_specs=(head_specs + [_const_spec((1, d)), _const_spec(w_in.shape, single_buffer=True)]
                  + [_const_spec(p.shape) for p in mix_params]
                  + [_const_spec(w_out.shape, single_buffer=True), _const_spec((1, d)),
                     _const_spec(wh.shape), _const_spec(wl.shape), _const_spec(br.shape)]),
        out_specs=[row(d), row(d), row(LANES)],
        out_shape=[jax.ShapeDtypeStruct((t, d), F32),
                   jax.ShapeDtypeStruct((t, d), F32),
                   jax.ShapeDtypeStruct((t, LANES), F32)],
        scratch_shapes=scratch,
        compiler_params=_params(1, nbytes),
        name=name,
    )(*head, g_mix.reshape(1, d), w_in, *mix_params, w_out, *tail)


RANK_BLOCK = 256


def _route_block(logits, run, before):
    n = logits.shape[1]
    row8 = lax.broadcasted_iota(I32, (SUBLANES, n), 0)
    row = lax.broadcasted_iota(I32, (N_EXPERTS, n), 0)

    gl = jnp.where(row8 < N_GROUPS, logits[0:SUBLANES], NEG_BIG)
    gmax = jnp.max(gl, axis=0, keepdims=True)
    gidx = jnp.min(jnp.where(gl == gmax, row8, N_EXPERTS), axis=0, keepdims=True)
    p_group = 1.0 / jnp.sum(jnp.exp(gl - gmax), axis=0, keepdims=True)

    el = jnp.where(row // EXPERTS_PER_GROUP == gidx, logits[SUBLANES:ROUTER_ROWS], NEG_BIG)
    m0 = jnp.max(el, axis=0, keepdims=True)
    e0 = jnp.min(jnp.where(el == m0, row, N_EXPERTS), axis=0, keepdims=True)
    el1 = jnp.where(row == e0, NEG_BIG, el)
    m1 = jnp.max(el1, axis=0, keepdims=True)
    e1 = jnp.min(jnp.where(el1 == m1, row, N_EXPERTS), axis=0, keepdims=True)
    ratio = jnp.exp(m1 - m0)
    gate0 = p_group / (1.0 + ratio)
    gate1 = p_group * ratio / (1.0 + ratio)

    sel0 = row == e0
    sel1 = row == e1
    pre0 = jnp.dot(sel0.astype(BF16), before, preferred_element_type=F32)
    pre1 = jnp.dot(sel1.astype(BF16), before, preferred_element_type=F32)
    cnt0 = jnp.sum(sel0.astype(F32), axis=1, keepdims=True)
    cnt1 = jnp.sum(sel1.astype(F32), axis=1, keepdims=True)
    rank0 = jnp.sum(jnp.where(sel0, pre0 + run, 0.0), axis=0, keepdims=True)
    rank1 = jnp.sum(jnp.where(sel1, pre1 + (run + cnt0), 0.0), axis=0, keepdims=True)

    meta = jnp.where(row8 == 0, e0, jnp.where(row8 == 1, e1, jnp.where(
        row8 == 2, rank0.astype(I32), jnp.where(row8 == 3, rank1.astype(I32), 0))))
    gates = jnp.where(row8 == 0, gate0, jnp.where(row8 == 1, gate1, 0.0))
    return meta, gates, run + (cnt0 + cnt1)


def _route_body(lg_ref, meta_ref, gate_ref, cnt_ref, run_ref):
    tr = lg_ref.shape[0]
    nb = min(RANK_BLOCK, tr)

    @pl.when(pl.program_id(0) == 0)
    def _():
        run_ref[...] = jnp.zeros(run_ref.shape, F32)

    src = lax.broadcasted_iota(I32, (nb, nb), 0)
    dst = lax.broadcasted_iota(I32, (nb, nb), 1)
    before = (src < dst).astype(BF16)
    run = run_ref[:, 0:1]
    for b in range(tr // nb):
        cols = slice(b * nb, (b + 1) * nb)
        logits_t = lg_ref[cols, :].T[0:ROUTER_ROWS]
        meta, gates, run = _route_block(logits_t, run, before)
        meta_ref[:, cols] = meta
        gate_ref[:, cols] = gates
    run_ref[...] = jnp.broadcast_to(run, run_ref.shape)
    cnt_ref[...] = run_ref[...]


def _route(logits, *, tr):
    t = logits.shape[0]
    return pl.pallas_call(
        _route_body,
        grid=(t // tr,),
        in_specs=[pl.BlockSpec((tr, LANES), lambda i: (i, 0))],
        out_specs=[pl.BlockSpec((SUBLANES, tr), lambda i: (0, i)),
                   pl.BlockSpec((SUBLANES, tr), lambda i: (0, i)),
                   pl.BlockSpec((N_EXPERTS, LANES), lambda i: (0, 0))],
        out_shape=[jax.ShapeDtypeStruct((SUBLANES, t), I32),
                   jax.ShapeDtypeStruct((SUBLANES, t), F32),
                   jax.ShapeDtypeStruct((N_EXPERTS, LANES), F32)],
        scratch_shapes=[pltpu.VMEM((N_EXPERTS, LANES), F32)],
        compiler_params=_params(1, 64 * tr * 4 * 4),
        name="route",
    )(logits)


def _pos_spec(tm):
    return pl.BlockSpec((1, 1, TOP_K * tm), lambda i: (i, 0, 0), memory_space=pltpu.SMEM)


def _wait_rows(src_ref, dst_ref, sem):
    pltpu.make_async_copy(src_ref, dst_ref, sem).wait()


def _scatter_body(pos_ref, h_ref, h_hbm, xs_ref, sem):
    tm = h_ref.shape[0]
    row0 = pl.program_id(0) * tm

    for r in range(tm):
        sources = (h_ref.at[pl.ds(r, 1), :], h_hbm.at[pl.ds(row0 + r, 1), :])
        for k in range(TOP_K):
            p = pos_ref[0, 0, k * tm + r]
            pltpu.make_async_copy(sources[k], xs_ref.at[pl.ds(p, 1), :], sem.at[k]).start()
    _wait_rows(h_ref, xs_ref.at[pl.ds(0, tm), :], sem.at[0])
    _wait_rows(h_hbm.at[pl.ds(0, tm), :], xs_ref.at[pl.ds(0, tm), :], sem.at[1])


def _scatter_rows(hn, pos3, *, tm):
    t, d = hn.shape
    return pl.pallas_call(
        _scatter_body,
        grid=(t // tm,),
        in_specs=[_pos_spec(tm), pl.BlockSpec((tm, d), lambda i: (i, 0)),
                  pl.BlockSpec(memory_space=pl.ANY)],
        out_specs=pl.BlockSpec(memory_space=pl.ANY),
        out_shape=jax.ShapeDtypeStruct((TOP_K * t, d), hn.dtype),
        scratch_shapes=[pltpu.SemaphoreType.DMA((TOP_K,))],
        compiler_params=_params(1, 2 * tm * d * 4),
        name="scatter_rows",
    )(pos3, hn, hn)


def _expert_body(vblk, vexp, vfirst, vnew, vvalid, vslot, vnext, vhalf, estart,
                 x_ref, w1_hbm, w3_hbm, w2_hbm, o_ref, wf1, wf3, wf2, w1b, w3b, w2b, sem, *, layer):
    bm = x_ref.shape[0]
    v = pl.program_id(0)
    weights = ((w1_hbm, wf1), (w3_hbm, wf3), (w2_hbm, wf2))

    def weight_copies(e, slot):
        return [pltpu.make_async_copy(w_hbm.at[layer, e], wf.at[slot], sem.at[slot, j])
                for j, (w_hbm, wf) in enumerate(weights)]

    @pl.when(v == 0)
    def _():
        for c in weight_copies(vexp[0], 0):
            c.start()

    @pl.when(vvalid[v] == 1)
    def _():
        @pl.when(vnew[v] == 1)
        def _():
            slot = vslot[v]
            for c in weight_copies(vexp[v], slot):
                c.wait()

            @pl.when(vnext[v] >= 0)
            def _():
                for c in weight_copies(vnext[v], 1 - slot):
                    c.start()

            w1b[...] = wf1[slot].astype(BF16)
            w3b[...] = wf3[slot].astype(BF16)
            w2b[...] = wf2[slot].astype(BF16)

        def visit(r0, n, r0_other, n_other):
            rows = slice(r0, r0 + n)
            xb = x_ref[rows, :].astype(BF16)
            h1 = jnp.dot(xb, w1b[...], preferred_element_type=F32)
            h3 = jnp.dot(xb, w3b[...], preferred_element_type=F32)
            hid = (h1 * _sigmoid(h1)) * h3
            y = jnp.dot(hid.astype(BF16), w2b[...], preferred_element_type=F32)

            @pl.when(vfirst[v] == 1)
            def _():
                o_ref[rows, :] = y
                if n_other:
                    o_ref[r0_other:r0_other + n_other, :] = jnp.zeros((n_other, o_ref.shape[1]), F32)

            @pl.when(vfirst[v] == 0)
            def _():
                e = vexp[v]
                r = vblk[v] * bm + r0 + lax.broadcasted_iota(I32, (n, 1), 0)
                mine = (r >= estart[e]) & (r < estart[e + 1])
                o_ref[rows, :] = jnp.where(mine, y, o_ref[rows, :])

        hb = bm // 2

        @pl.when(vhalf[v] == 0)
        def _():
            visit(0, bm, 0, 0)

        @pl.when(vhalf[v] == 1)
        def _():
            visit(0, hb, hb, bm - hb)

        @pl.when(vhalf[v] == 2)
        def _():
            visit(hb, bm - hb, 0, hb)


def _experts(xs, w1, w3, w2, tables, *, layer, bm):
    n, d = xs.shape
    f = w1.shape[3]
    n_visits = tables[0].shape[0]
    nbytes = 2 * (bm * d * 4 * 2 + 3 * d * f * 4) + 3 * d * f * 2 + 6 * bm * d * 4
    hbm = pl.BlockSpec(memory_space=pl.ANY)
    grid_spec = pltpu.PrefetchScalarGridSpec(
        num_scalar_prefetch=len(tables),
        grid=(n_visits,),
        in_specs=[pl.BlockSpec((bm, d), lambda v, blk, *_: (blk[v], 0)), hbm, hbm, hbm],
        out_specs=pl.BlockSpec((bm, d), lambda v, blk, *_: (blk[v], 0)),
        scratch_shapes=[pltpu.VMEM((2, d, f), F32), pltpu.VMEM((2, d, f), F32), pltpu.VMEM((2, f, d), F32),
                        pltpu.VMEM((d, f), BF16), pltpu.VMEM((d, f), BF16), pltpu.VMEM((f, d), BF16),
                        pltpu.SemaphoreType.DMA((2, 3))])
    return pl.pallas_call(
        functools.partial(_expert_body, layer=layer),
        grid_spec=grid_spec,
        out_shape=jax.ShapeDtypeStruct((n, d), F32),
        compiler_params=_params(1, nbytes),
        name="experts",
    )(*tables, xs, w1, w3, w2)


def _visit_tables(counts, n_slots, bm):
    n_blocks = n_slots // bm
    n_visits = n_blocks + N_EXPERTS - 1
    end = jnp.cumsum(counts)
    start = end - counts
    first_blk = start // bm
    last_blk = jnp.maximum(end - 1, 0) // bm
    per_e = jnp.where(counts > 0, last_blk - first_blk + 1, 0)
    cum = jnp.cumsum(per_e)
    total = cum[-1]
    v = jnp.minimum(jnp.arange(n_visits, dtype=I32), total - 1)
    e = jnp.sum((cum[None, :] <= v[:, None]).astype(I32), axis=1)
    blk = (first_blk[e] + (v - (cum[e] - per_e[e]))).astype(I32)
    valid = (jnp.arange(n_visits) < total).astype(I32)
    prev_blk = jnp.concatenate([jnp.full((1,), -1, I32), blk[:-1]])
    prev_e = jnp.concatenate([jnp.full((1,), -1, I32), e[:-1]])
    first = (blk != prev_blk).astype(I32)
    new = (e != prev_e).astype(I32)
    ordinal = jnp.cumsum(new) - 1
    slot = (ordinal % 2).astype(I32)
    is_next = (new[None, :] == 1) & (ordinal[None, :] == ordinal[:, None] + 1)
    nxt = jnp.where(jnp.any(is_next, axis=1), jnp.sum(jnp.where(is_next, e[None, :], 0), axis=1), -1)
    lo = jnp.maximum(start[e], blk * bm) - blk * bm
    hi = jnp.minimum(end[e], (blk + 1) * bm) - blk * bm
    half = jnp.where(hi <= bm // 2, 1, jnp.where(lo >= bm // 2, 2, 0)).astype(I32)
    estart = jnp.concatenate([start, end[-1:]]).astype(I32)
    return blk, e, first, new, valid, slot, nxt.astype(I32), half, estart


def _combine_body(pos_ref, x_ref, gt_ref, g_ref, ys_ref, o_ref, ybuf, sem):
    tm = x_ref.shape[0]
    for r in range(tm):
        for k in range(TOP_K):
            p = pos_ref[0, 0, k * tm + r]
            pltpu.make_async_copy(ys_ref.at[pl.ds(p, 1), :], ybuf.at[k, pl.ds(r, 1), :],
                                  sem).start(priority=k)
    for k in range(TOP_K):
        _wait_rows(ys_ref.at[pl.ds(0, tm), :], ybuf.at[k], sem)
    x2 = x_ref[...] + gt_ref[:, 0:1] * ybuf[0] + gt_ref[:, 1:2] * ybuf[1]
    o_ref[...] = _rms_norm(x2, g_ref[...])


def _combine(x1, ys, pos3, gates_t, g_final, *, tm):
    t, d = x1.shape
    nbytes = 2 * (2 * tm * d * 4 + tm * LANES * 4) + TOP_K * tm * d * 4 + 2 * tm * d * 4
    return pl.pallas_call(
        _combine_body,
        grid=(t // tm,),
        in_specs=[_pos_spec(tm),
                  pl.BlockSpec((tm, d), lambda i: (i, 0)),
                  pl.BlockSpec((tm, TOP_K), lambda i: (i, 0)),
                  pl.BlockSpec((1, d), lambda i: (0, 0)),
                  pl.BlockSpec(memory_space=pl.ANY)],
        out_specs=pl.BlockSpec((tm, d), lambda i: (i, 0)),
        out_shape=jax.ShapeDtypeStruct((t, d), F32),
        scratch_shapes=[pltpu.VMEM((TOP_K, tm, d), F32), pltpu.SemaphoreType.DMA],
        compiler_params=_params(1, nbytes),
        name="combine",
    )(pos3, x1, gates_t, g_final.reshape(1, d), ys)


def _tiles(t):
    return dict(tm_front=min(256, t), tr=min(2048, t), tm_rows=min(256, t), bm=min(256, TOP_K * t))


def _router_params(w_group, b_group, w_router, b_router):
    d = w_group.shape[0]
    gpad = SUBLANES - N_GROUPS
    wr = jnp.concatenate([w_group, jnp.zeros((d, gpad), F32), w_router,
                          jnp.zeros((d, LANES - ROUTER_ROWS), F32)], axis=1)
    br = jnp.concatenate([b_group, jnp.zeros((gpad,), F32), b_router,
                          jnp.zeros((LANES - ROUTER_ROWS,), F32)]).reshape(1, LANES)
    return wr, br


def _moe(x1, hn, logits, w1, w3, w2, layer, tl):
    t, d = x1.shape
    meta, gates, cnt = _route(logits, tr=tl["tr"])

    counts = cnt[:, 0].astype(I32)
    start = jnp.cumsum(counts) - counts
    is_e = meta[0:TOP_K, :, None] == jnp.arange(N_EXPERTS, dtype=I32)
    pos = jnp.sum(jnp.where(is_e, start, 0), axis=-1) + meta[TOP_K:2 * TOP_K]
    tm = tl["tm_rows"]
    pos3 = pos.reshape(TOP_K, t // tm, tm).transpose(1, 0, 2).reshape(t // tm, 1, TOP_K * tm)

    xs = _scatter_rows(hn, pos3, tm=tm)
    tables = _visit_tables(counts, TOP_K * t, tl["bm"])
    ys = _experts(xs, w1, w3, w2, tables, layer=layer, bm=tl["bm"])
    return x1, gates[0:TOP_K].T, pos3, ys


def kernel(x, ln_mix, ln_ffn, ln_final, ev_w_in, ev_conv_w, ev_sgu_ln_g, ev_sgu_ln_b, ev_sgu_w, ev_sgu_b, ev_w_out, od_w_in, od_pool_w, od_pool_scale, od_dw_w, od_dw_b, od_ln_g, od_ln_b, od_w_out, moe_w_group, moe_b_group, moe_w_router, moe_b_router, moe_w1, moe_w3, moe_w2):
    bsz, s, d = x.shape
    assert bsz == 1, "token mixers carry state across row tiles of one sequence"
    depth = ln_mix.shape[0]
    tl = _tiles(s)
    tm = tl["tm_front"]
    assert tm == tl["tm_rows"], "the fused gather reads pos3 in front-kernel tiles"
    x_in = x.reshape(s, d)
    for layer in range(depth):
        i = layer // 2
        wr, br = _router_params(moe_w_group[layer], moe_b_group[layer],
                                moe_w_router[layer], moe_b_router[layer])
        if layer % 2 == 0:
            c = ev_conv_w.shape[2]
            mix_params = [ev_conv_w[i], ev_sgu_ln_g[i].reshape(1, c), ev_sgu_ln_b[i].reshape(1, c),
                          ev_sgu_w[i], ev_sgu_b[i].T]
            mix_scratch = [pltpu.VMEM((tm + EVEN_HALO, c), F32)]
            x1, hn, logits = _layer_front(
                "even_layer", _even_mix, x_in, ln_mix[layer], ev_w_in[i].astype(BF16), mix_params,
                mix_scratch, ev_w_out[i].astype(BF16), ln_ffn[layer], wr, br, tm=tm)
        else:
            c = od_dw_w.shape[2]
            mix_params = [od_pool_w[i], od_pool_scale[i].reshape(1, c), od_dw_w[i],
                          od_dw_b[i].reshape(1, c), od_ln_g[i].reshape(1, c), od_ln_b[i].reshape(1, c)]
            gd = c // len(POOL_WINDOWS)
            mix_scratch = [pltpu.VMEM((tm + POOL_HALO, c), F32),
                           pltpu.VMEM((tm + POOL_HALO, gd), F32), pltpu.VMEM((tm + POOL_HALO, gd), F32),
                           pltpu.VMEM((tm + CONF_HALO, c), F32),
                           pltpu.VMEM((SUBLANES - 1, tm + CONF_HALO, LANES), F32),
                           pltpu.VMEM((tm, c), F32)]
            x1, hn, logits = _layer_front(
                "odd_layer", _odd_mix, x_in, ln_mix[layer], od_w_in[i].astype(BF16), mix_params,
                mix_scratch, od_w_out[i].astype(BF16), ln_ffn[layer], wr, br, tm=tm)
        x_in = _moe(x1, hn, logits, moe_w1, moe_w3, moe_w2, layer, tl)
    x1, gates_t, pos3, ys = x_in
    out = _combine(x1, ys, pos3, gates_t, ln_final, tm=tl["tm_rows"])
    return out.reshape(bsz, s, d)
```

```python
import functools
import math

import jax
import jax.numpy as jnp
from jax import lax
from jax.experimental import pallas as pl
from jax.experimental.pallas import tpu as pltpu

F32 = jnp.float32
BF16 = jnp.bfloat16
I32 = jnp.int32

EPS = 1e-6
CHUNK = 64
SGU_BLOCK = 128
SGU_HEADS = 8
POOL_WINDOWS = (2, 4, 8, 16)
N_GROUPS = 4
EXPERTS_PER_GROUP = 8
N_EXPERTS = N_GROUPS * EXPERTS_PER_GROUP
TOP_K = 2

LANES = 128
SUBLANES = 8
VMEM_CAP_BYTES = 60 * 1024 * 1024

ROUTER_ROWS = SUBLANES + N_EXPERTS
NEG_BIG = -1e30


def _vmem_limit(nbytes):
    return int(min(VMEM_CAP_BYTES, nbytes * 5 // 4 + (6 << 20)))


def _params(n_axes, nbytes):
    return pltpu.CompilerParams(
        dimension_semantics=("arbitrary",) * n_axes,
        vmem_limit_bytes=_vmem_limit(nbytes))


def _const_spec(shape, single_buffer=False):
    mode = dict(pipeline_mode=pl.Buffered(1)) if single_buffer else {}
    return pl.BlockSpec(shape, lambda i: (0,) * len(shape), **mode)


def _sigmoid(x):
    return 1.0 / (1.0 + jnp.exp(-x))


def _gelu_tanh(x):
    c = math.sqrt(2.0 / math.pi)
    return x * (0.5 * (1.0 + jnp.tanh(c * (x + 0.044715 * (x * x * x)))))


def _layer_norm(v, g, b):
    mu = jnp.mean(v, axis=-1, keepdims=True)
    vc = v - mu
    var = jnp.mean(vc * vc, axis=-1, keepdims=True)
    return vc * lax.rsqrt(var + EPS) * g + b


def _rms_norm(x, g):
    ms = jnp.mean(x * x, axis=-1, keepdims=True)
    return x * lax.rsqrt(ms + EPS) * g


def _split_bf16(v):
    head = v.astype(BF16)
    return head, (v - head.astype(F32)).astype(BF16)


IN_PROJ_COLS = 1024


def _in_proj(x_ref, g_ref, w_ref, z_ref):
    h = _rms_norm(x_ref[...], g_ref[...]).astype(BF16)
    for c0 in range(0, w_ref.shape[1], IN_PROJ_COLS):
        cols = slice(c0, c0 + IN_PROJ_COLS)
        z_ref[:, cols] = jnp.dot(h, w_ref[:, cols], preferred_element_type=F32)


def _out_proj_route(x_ref, y_ref, w_ref, g_ref, wh_ref, wl_ref, br_ref, x1_ref, hn_ref, lg_ref):
    x1 = x_ref[...] + jnp.dot(y_ref[...], w_ref[...], preferred_element_type=F32)
    x1_ref[...] = x1
    hn = _rms_norm(x1, g_ref[...])
    hn_ref[...] = hn
    head, rem = _split_bf16(hn)
    lg_ref[...] = (jnp.dot(head, wh_ref[...], preferred_element_type=F32)
                   + jnp.dot(rem, wh_ref[...], preferred_element_type=F32)
                   + jnp.dot(head, wl_ref[...], preferred_element_type=F32)
                   + br_ref[...])


EVEN_HALO = SUBLANES


def _even_mix(z_ref, cw_ref, lg_ref, lb_ref, sw_ref, sb_ref, o_ref, pbuf):
    tm = z_ref.shape[0]
    kw, c = cw_ref.shape
    col = lambda j: z_ref[:, j * c:(j + 1) * c]

    @pl.when(pl.program_id(0) == 0)
    def _():
        pbuf[0:EVEN_HALO, :] = jnp.zeros((EVEN_HALO, c), F32)

    p = col(1) * col(2)
    pbuf[EVEN_HALO:EVEN_HALO + tm, :] = p
    conv = cw_ref[kw - 1:kw, :] * p
    for k in range(kw - 1):
        conv = conv + cw_ref[k:k + 1, :] * pbuf[pl.ds(EVEN_HALO - (kw - 1) + k, tm), :]
    o_ref[:, 0:c] = (col(0) * conv).astype(BF16)
    pbuf[0:EVEN_HALO, :] = pbuf[tm:tm + EVEN_HALO, :]

    u = _gelu_tanh(col(3))
    vn = _layer_norm(_gelu_tanh(col(4)), lg_ref[...], lb_ref[...]).astype(BF16)
    tpos = lax.broadcasted_iota(I32, (SGU_BLOCK, SGU_BLOCK), 0)
    spos = lax.broadcasted_iota(I32, (SGU_BLOCK, SGU_BLOCK), 1)
    mask = (spos // CHUNK) <= (tpos // CHUNK)
    hd = c // SGU_HEADS
    for h in range(SGU_HEADS):
        wh = jnp.where(mask, sw_ref[h], 0.0).astype(BF16)
        bh = sb_ref[:, h:h + 1]
        cols = slice(h * hd, (h + 1) * hd)
        for n in range(tm // SGU_BLOCK):
            rows = slice(n * SGU_BLOCK, (n + 1) * SGU_BLOCK)
            sv = jnp.dot(wh, vn[rows, cols], preferred_element_type=F32) + bh
            o_ref[rows, c + h * hd:c + (h + 1) * hd] = (u[rows, cols] * sv).astype(BF16)


POOL_HALO = 24
CONF_HALO = 32
CONV_ROWS = 128


def _window_sum(xbuf, cols, w, lv_a, lv_b, tm):
    lo, hi = SUBLANES, POOL_HALO + tm
    src, src_cols, n = xbuf, cols, 1
    dst, other = lv_a, lv_b
    while 2 * n < w:
        dst[lo:hi, :] = src[lo:hi, src_cols] + src[pl.ds(lo - n, hi - lo), src_cols]
        src, src_cols, n = dst, slice(None), 2 * n
        dst, other = other, dst
    return src[POOL_HALO:hi, src_cols] + src[pl.ds(POOL_HALO - n, tm), src_cols]


def _odd_mix(z_ref, pw_ref, ps_ref, dw_ref, db_ref, lg_ref, lb_ref, o_ref,
             xbuf, lv_a, lv_b, gbuf, sbuf, dbuf):
    tm = z_ref.shape[0]
    kw, c = dw_ref.shape
    i = pl.program_id(0)

    @pl.when(i == 0)
    def _():
        xbuf[0:POOL_HALO, :] = jnp.zeros((POOL_HALO, c), F32)
        gbuf[0:CONF_HALO, :] = jnp.zeros((CONF_HALO, c), F32)
        lv_a[0:SUBLANES, :] = jnp.zeros((SUBLANES, lv_a.shape[1]), F32)
        lv_b[0:SUBLANES, :] = jnp.zeros((SUBLANES, lv_b.shape[1]), F32)

    x = z_ref[:, 0:c]
    xbuf[POOL_HALO:POOL_HALO + tm, :] = x
    step = i * tm + lax.broadcasted_iota(I32, (tm, 1), 0) + 1
    gd = c // len(POOL_WINDOWS)
    for g, w in enumerate(POOL_WINDOWS):
        cols = slice(g * gd, (g + 1) * gd)
        s = _window_sum(xbuf, cols, w, lv_a, lv_b, tm)
        inv_count = 1.0 / jnp.minimum(step, w).astype(F32)
        d = s * inv_count - x[:, cols]
        y = jnp.dot(d.astype(BF16), pw_ref[g].astype(BF16), preferred_element_type=F32)
        o_ref[:, cols] = (y * ps_ref[:, cols]).astype(BF16)
    xbuf[0:POOL_HALO, :] = xbuf[tm:tm + POOL_HALO, :]

    gbuf[CONF_HALO:CONF_HALO + tm, :] = z_ref[:, c:2 * c] * _sigmoid(z_ref[:, 2 * c:3 * c])
    n_rows = CONF_HALO + tm

    def conv_cols(cb, carry):
        lanes = pl.ds(pl.multiple_of(cb * LANES, LANES), LANES)
        for b in range(1, SUBLANES):
            sbuf[b - 1, SUBLANES:n_rows, :] = gbuf[pl.ds(SUBLANES - b, n_rows - SUBLANES), lanes]
        accs = [jnp.broadcast_to(db_ref[:, lanes], (CONV_ROWS, LANES)) for _ in range(0, tm, CONV_ROWS)]
        for back in range(kw):
            a, b = divmod(back, SUBLANES)
            wk = dw_ref[kw - 1 - back:kw - back, lanes]
            for n, r0 in enumerate(range(0, tm, CONV_ROWS)):
                start = r0 + CONF_HALO - SUBLANES * a
                tap = gbuf[start:start + CONV_ROWS, lanes] if b == 0 else sbuf[b - 1, start:start + CONV_ROWS, :]
                accs[n] = accs[n] + wk * tap
        for n, r0 in enumerate(range(0, tm, CONV_ROWS)):
            dbuf[r0:r0 + CONV_ROWS, lanes] = accs[n]
        return carry

    lax.fori_loop(0, c // LANES, conv_cols, 0)
    gbuf[0:CONF_HALO, :] = gbuf[tm:tm + CONF_HALO, :]
    yn = _layer_norm(dbuf[...], lg_ref[...], lb_ref[...])
    o_ref[:, c:2 * c] = (yn * _sigmoid(yn)).astype(BF16)


def _start_row_gathers(pos_ref, ys_ref, ybuf, sem, slot):
    tm = ybuf.shape[2]
    for r in range(tm):
        for k in range(TOP_K):
            p = pos_ref[0, 0, k * tm + r]
            pltpu.make_async_copy(ys_ref.at[pl.ds(p, 1), :], ybuf.at[slot, k, pl.ds(r, 1), :],
                                  sem.at[slot]).start(priority=k)


def _wait_row_gathers(ys_ref, ybuf, sem, slot):
    tm = ybuf.shape[2]
    for k in range(TOP_K):
        _wait_rows(ys_ref.at[pl.ds(0, tm), :], ybuf.at[slot, k], sem.at[slot])


def _gathered_input(pos_cur, pos_next, x1_ref, gt_ref, ys_ref, x_ref, ybuf, sem):
    i = pl.program_id(0)
    slot = i % 2

    @pl.when(i == 0)
    def _():
        _start_row_gathers(pos_cur, ys_ref, ybuf, sem, 0)

    _wait_row_gathers(ys_ref, ybuf, sem, slot)
    x_ref[...] = x1_ref[...] + gt_ref[:, 0:1] * ybuf[slot, 0] + gt_ref[:, 1:2] * ybuf[slot, 1]
    _start_row_gathers(pos_next, ys_ref, ybuf, sem, 1 - slot)


def _front_body(*refs, mix, n_mix_params, gathered):
    n_head = 5 if gathered else 1
    head, refs = refs[:n_head], refs[n_head:]
    gm_ref, win_ref = refs[0:2]
    mix_params = refs[2:2 + n_mix_params]
    wout_ref, gf_ref, wh_ref, wl_ref, br_ref, x1_ref, hn_ref, lgt_ref, z_ref, y_ref = (
        refs[2 + n_mix_params:12 + n_mix_params])
    scratch = refs[12 + n_mix_params:]
    if gathered:
        mix_scratch, (x_ref, ybuf, sem) = scratch[:-3], scratch[-3:]
        _gathered_input(*head, x_ref, ybuf, sem)
    else:
        mix_scratch, x_ref = scratch, head[0]
    _in_proj(x_ref, gm_ref, win_ref, z_ref)
    mix(z_ref, *mix_params, y_ref, *mix_scratch)
    _out_proj_route(x_ref, y_ref, wout_ref, gf_ref, wh_ref, wl_ref, br_ref, x1_ref, hn_ref, lgt_ref)
    if gathered:
        @pl.when(pl.program_id(0) == pl.num_programs(0) - 1)
        def _():
            _wait_row_gathers(head[4], ybuf, sem, 1 - pl.program_id(0) % 2)


def _layer_front(name, mix, x_in, g_mix, w_in, mix_params, mix_scratch, w_out, g_ffn, wr, br, *, tm):
    gathered = isinstance(x_in, tuple)
    t, d = (x_in[0] if gathered else x_in).shape
    nt = t // tm
    n = w_in.shape[1]
    c2 = w_out.shape[0]
    wh, wl = _split_bf16(wr)
    row = lambda width: pl.BlockSpec((tm, width), lambda i: (i, 0))
    tail = [g_ffn.reshape(1, d), wh, wl, br]
    scratch = [pltpu.VMEM((tm, n), F32), pltpu.VMEM((tm, c2), BF16)] + mix_scratch
    if gathered:
        x1, gates_t, pos3, ys = x_in
        head = [pos3, pos3, x1, gates_t, ys]
        head_specs = [_pos_spec(tm),
                      pl.BlockSpec((1, 1, TOP_K * tm), lambda i: (jnp.minimum(i + 1, nt - 1), 0, 0),
                                   memory_space=pltpu.SMEM),
                      row(d), pl.BlockSpec((tm, TOP_K), lambda i: (i, 0)),
                      pl.BlockSpec(memory_space=pl.ANY)]
        scratch += [pltpu.VMEM((tm, d), F32), pltpu.VMEM((2, TOP_K, tm, d), F32),
                    pltpu.SemaphoreType.DMA((2,))]
    else:
        head, head_specs = [x_in], [row(d)]
    nbytes = (d * n * 2 + c2 * d * 2
              + 2 * 3 * tm * d * 4 + 2 * tm * LANES * 4
              + tm * n * 4 + tm * c2 * 2
              + sum(math.prod(s.shape) * 4 for s in mix_scratch)
              + 2 * sum(p.size * p.dtype.itemsize for p in mix_params + tail[1:])
              + (2 * TOP_K + 1) * tm * d * 4 * gathered
              + 6 * tm * d * 4)
    return pl.pallas_call(
        functools.partial(_front_body, mix=mix, n_mix_params=len(mix_params), gathered=gathered),
        grid=(nt,),
        in_specs=(head_specs + [_const_spec((1, d)), _const_spec(w_in.shape, single_buffer=True)]
                  + [_const_spec(p.shape) for p in mix_params]
                  + [_const_spec(w_out.shape, single_buffer=True), _const_spec((1, d)),
                     _const_spec(wh.shape), _const_spec(wl.shape), _const_spec(br.shape)]),
        out_specs=[row(d), row(d), row(LANES)],
        out_shape=[jax.ShapeDtypeStruct((t, d), F32),
                   jax.ShapeDtypeStruct((t, d), F32),
                   jax.ShapeDtypeStruct((t, LANES), F32)],
        scratch_shapes=scratch,
        compiler_params=_params(1, nbytes),
        name=name,
    )(*head, g_mix.reshape(1, d), w_in, *mix_params, w_out, *tail)


RANK_BLOCK = 256


def _route_block(logits, run, before):
    n = logits.shape[1]
    row8 = lax.broadcasted_iota(I32, (SUBLANES, n), 0)
    row = lax.broadcasted_iota(I32, (N_EXPERTS, n), 0)

    gl = jnp.where(row8 < N_GROUPS, logits[0:SUBLANES], NEG_BIG)
    gmax = jnp.max(gl, axis=0, keepdims=True)
    gidx = jnp.min(jnp.where(gl == gmax, row8, N_EXPERTS), axis=0, keepdims=True)
    p_group = 1.0 / jnp.sum(jnp.exp(gl - gmax), axis=0, keepdims=True)

    el = jnp.where(row // EXPERTS_PER_GROUP == gidx, logits[SUBLANES:ROUTER_ROWS], NEG_BIG)
    m0 = jnp.max(el, axis=0, keepdims=True)
    e0 = jnp.min(jnp.where(el == m0, row, N_EXPERTS), axis=0, keepdims=True)
    el1 = jnp.where(row == e0, NEG_BIG, el)
    m1 = jnp.max(el1, axis=0, keepdims=True)
    e1 = jnp.min(jnp.where(el1 == m1, row, N_EXPERTS), axis=0, keepdims=True)
    ratio = jnp.exp(m1 - m0)
    gate0 = p_group / (1.0 + ratio)
    gate1 = p_group * ratio / (1.0 + ratio)

    sel0 = row == e0
    sel1 = row == e1
    pre0 = jnp.dot(sel0.astype(BF16), before, preferred_element_type=F32)
    pre1 = jnp.dot(sel1.astype(BF16), before, preferred_element_type=F32)
    cnt0 = jnp.sum(sel0.astype(F32), axis=1, keepdims=True)
    cnt1 = jnp.sum(sel1.astype(F32), axis=1, keepdims=True)
    rank0 = jnp.sum(jnp.where(sel0, pre0 + run, 0.0), axis=0, keepdims=True)
    rank1 = jnp.sum(jnp.where(sel1, pre1 + (run + cnt0), 0.0), axis=0, keepdims=True)

    meta = jnp.where(row8 == 0, e0, jnp.where(row8 == 1, e1, jnp.where(
        row8 == 2, rank0.astype(I32), jnp.where(row8 == 3, rank1.astype(I32), 0))))
    gates = jnp.where(row8 == 0, gate0, jnp.where(row8 == 1, gate1, 0.0))
    return meta, gates, run + (cnt0 + cnt1)


def _route_body(lg_ref, meta_ref, gate_ref, cnt_ref, run_ref):
    tr = lg_ref.shape[0]
    nb = min(RANK_BLOCK, tr)

    @pl.when(pl.program_id(0) == 0)
    def _():
        run_ref[...] = jnp.zeros(run_ref.shape, F32)

    src = lax.broadcasted_iota(I32, (nb, nb), 0)
    dst = lax.broadcasted_iota(I32, (nb, nb), 1)
    before = (src < dst).astype(BF16)
    run = run_ref[:, 0:1]
    for b in range(tr // nb):
        cols = slice(b * nb, (b + 1) * nb)
        logits_t = lg_ref[cols, :].T[0:ROUTER_ROWS]
        meta, gates, run = _route_block(logits_t, run, before)
        meta_ref[:, cols] = meta
        gate_ref[:, cols] = gates
    run_ref[...] = jnp.broadcast_to(run, run_ref.shape)
    cnt_ref[...] = run_ref[...]


def _route(logits, *, tr):
    t = logits.shape[0]
    return pl.pallas_call(
        _route_body,
        grid=(t // tr,),
        in_specs=[pl.BlockSpec((tr, LANES), lambda i: (i, 0))],
        out_specs=[pl.BlockSpec((SUBLANES, tr), lambda i: (0, i)),
                   pl.BlockSpec((SUBLANES, tr), lambda i: (0, i)),
                   pl.BlockSpec((N_EXPERTS, LANES), lambda i: (0, 0))],
        out_shape=[jax.ShapeDtypeStruct((SUBLANES, t), I32),
                   jax.ShapeDtypeStruct((SUBLANES, t), F32),
                   jax.ShapeDtypeStruct((N_EXPERTS, LANES), F32)],
        scratch_shapes=[pltpu.VMEM((N_EXPERTS, LANES), F32)],
        compiler_params=_params(1, 64 * tr * 4 * 4),
        name="route",
    )(logits)


def _pos_spec(tm):
    return pl.BlockSpec((1, 1, TOP_K * tm), lambda i: (i, 0, 0), memory_space=pltpu.SMEM)


def _wait_rows(src_ref, dst_ref, sem):
    pltpu.make_async_copy(src_ref, dst_ref, sem).wait()


def _scatter_body(pos_ref, h_ref, xs_ref, sem):
    tm = h_ref.shape[0]

    for r in range(tm):
        for k in range(TOP_K):
            p = pos_ref[0, 0, k * tm + r]
            pltpu.make_async_copy(h_ref.at[pl.ds(r, 1), :], xs_ref.at[pl.ds(p, 1), :],
                                  sem).start(priority=k)
    for k in range(TOP_K):
        _wait_rows(h_ref, xs_ref.at[pl.ds(0, tm), :], sem)


def _scatter_rows(hn, pos3, *, tm):
    t, d = hn.shape
    return pl.pallas_call(
        _scatter_body,
        grid=(t // tm,),
        in_specs=[_pos_spec(tm), pl.BlockSpec((tm, d), lambda i: (i, 0))],
        out_specs=pl.BlockSpec(memory_space=pl.ANY),
        out_shape=jax.ShapeDtypeStruct((TOP_K * t, d), hn.dtype),
        scratch_shapes=[pltpu.SemaphoreType.DMA],
        compiler_params=_params(1, 2 * tm * d * 4),
        name="scatter_rows",
    )(pos3, hn)


def _expert_body(vblk, vexp, vfirst, vnew, vvalid, vslot, vnext, vhalf, estart,
                 x_ref, w1_hbm, w3_hbm, w2_hbm, o_ref, wf1, wf3, wf2, w1b, w3b, w2b, sem, *, layer):
    bm = x_ref.shape[0]
    v = pl.program_id(0)
    weights = ((w1_hbm, wf1), (w3_hbm, wf3), (w2_hbm, wf2))

    def weight_copies(e, slot):
        return [pltpu.make_async_copy(w_hbm.at[layer, e], wf.at[slot], sem.at[slot, j])
                for j, (w_hbm, wf) in enumerate(weights)]

    @pl.when(v == 0)
    def _():
        for c in weight_copies(vexp[0], 0):
            c.start()

    @pl.when(vvalid[v] == 1)
    def _():
        @pl.when(vnew[v] == 1)
        def _():
            slot = vslot[v]
            for c in weight_copies(vexp[v], slot):
                c.wait()

            @pl.when(vnext[v] >= 0)
            def _():
                for c in weight_copies(vnext[v], 1 - slot):
                    c.start()

            w1b[...] = wf1[slot].astype(BF16)
            w3b[...] = wf3[slot].astype(BF16)
            w2b[...] = wf2[slot].astype(BF16)

        def visit(r0, n, r0_other, n_other):
            rows = slice(r0, r0 + n)
            xb = x_ref[rows, :].astype(BF16)
            h1 = jnp.dot(xb, w1b[...], preferred_element_type=F32)
            h3 = jnp.dot(xb, w3b[...], preferred_element_type=F32)
            hid = (h1 * _sigmoid(h1)) * h3
            y = jnp.dot(hid.astype(BF16), w2b[...], preferred_element_type=F32)

            @pl.when(vfirst[v] == 1)
            def _():
                o_ref[rows, :] = y
                if n_other:
                    o_ref[r0_other:r0_other + n_other, :] = jnp.zeros((n_other, o_ref.shape[1]), F32)

            @pl.when(vfirst[v] == 0)
            def _():
                e = vexp[v]
                r = vblk[v] * bm + r0 + lax.broadcasted_iota(I32, (n, 1), 0)
                mine = (r >= estart[e]) & (r < estart[e + 1])
                o_ref[rows, :] = jnp.where(mine, y, o_ref[rows, :])

        hb = bm // 2

        @pl.when(vhalf[v] == 0)
        def _():
            visit(0, bm, 0, 0)

        @pl.when(vhalf[v] == 1)
        def _():
            visit(0, hb, hb, bm - hb)

        @pl.when(vhalf[v] == 2)
        def _():
            visit(hb, bm - hb, 0, hb)


def _experts(xs, w1, w3, w2, tables, *, layer, bm):
    n, d = xs.shape
    f = w1.shape[3]
    n_visits = tables[0].shape[0]
    nbytes = 2 * (bm * d * 4 * 2 + 3 * d * f * 4) + 3 * d * f * 2 + 6 * bm * d * 4
    hbm = pl.BlockSpec(memory_space=pl.ANY)
    grid_spec = pltpu.PrefetchScalarGridSpec(
        num_scalar_prefetch=len(tables),
        grid=(n_visits,),
        in_specs=[pl.BlockSpec((bm, d), lambda v, blk, *_: (blk[v], 0)), hbm, hbm, hbm],
        out_specs=pl.BlockSpec((bm, d), lambda v, blk, *_: (blk[v], 0)),
        scratch_shapes=[pltpu.VMEM((2, d, f), F32), pltpu.VMEM((2, d, f), F32), pltpu.VMEM((2, f, d), F32),
                        pltpu.VMEM((d, f), BF16), pltpu.VMEM((d, f), BF16), pltpu.VMEM((f, d), BF16),
                        pltpu.SemaphoreType.DMA((2, 3))])
    return pl.pallas_call(
        functools.partial(_expert_body, layer=layer),
        grid_spec=grid_spec,
        out_shape=jax.ShapeDtypeStruct((n, d), F32),
        compiler_params=_params(1, nbytes),
        name="experts",
    )(*tables, xs, w1, w3, w2)


def _visit_tables(counts, n_slots, bm):
    n_blocks = n_slots // bm
    n_visits = n_blocks + N_EXPERTS - 1
    end = jnp.cumsum(counts)
    start = end - counts
    first_blk = start // bm
    last_blk = jnp.maximum(end - 1, 0) // bm
    per_e = jnp.where(counts > 0, last_blk - first_blk + 1, 0)
    cum = jnp.cumsum(per_e)
    total = cum[-1]
    v = jnp.minimum(jnp.arange(n_visits, dtype=I32), total - 1)
    e = jnp.sum((cum[None, :] <= v[:, None]).astype(I32), axis=1)
    blk = (first_blk[e] + (v - (cum[e] - per_e[e]))).astype(I32)
    valid = (jnp.arange(n_visits) < total).astype(I32)
    prev_blk = jnp.concatenate([jnp.full((1,), -1, I32), blk[:-1]])
    prev_e = jnp.concatenate([jnp.full((1,), -1, I32), e[:-1]])
    first = (blk != prev_blk).astype(I32)
    new = (e != prev_e).astype(I32)
    ordinal = jnp.cumsum(new) - 1
    slot = (ordinal % 2).astype(I32)
    is_next = (new[None, :] == 1) & (ordinal[None, :] == ordinal[:, None] + 1)
    nxt = jnp.where(jnp.any(is_next, axis=1), jnp.sum(jnp.where(is_next, e[None, :], 0), axis=1), -1)
    lo = jnp.maximum(start[e], blk * bm) - blk * bm
    hi = jnp.minimum(end[e], (blk + 1) * bm) - blk * bm
    half = jnp.where(hi <= bm // 2, 1, jnp.where(lo >= bm // 2, 2, 0)).astype(I32)
    estart = jnp.concatenate([start, end[-1:]]).astype(I32)
    return blk, e, first, new, valid, slot, nxt.astype(I32), half, estart


def _combine_body(pos_ref, x_ref, gt_ref, g_ref, ys_ref, o_ref, ybuf, sem):
    tm = x_ref.shape[0]
    for r in range(tm):
        for k in range(TOP_K):
            p = pos_ref[0, 0, k * tm + r]
            pltpu.make_async_copy(ys_ref.at[pl.ds(p, 1), :], ybuf.at[k, pl.ds(r, 1), :],
                                  sem).start(priority=k)
    for k in range(TOP_K):
        _wait_rows(ys_ref.at[pl.ds(0, tm), :], ybuf.at[k], sem)
    x2 = x_ref[...] + gt_ref[:, 0:1] * ybuf[0] + gt_ref[:, 1:2] * ybuf[1]
    o_ref[...] = _rms_norm(x2, g_ref[...])


def _combine(x1, ys, pos3, gates_t, g_final, *, tm):
    t, d = x1.shape
    nbytes = 2 * (2 * tm * d * 4 + tm * LANES * 4) + TOP_K * tm * d * 4 + 2 * tm * d * 4
    return pl.pallas_call(
        _combine_body,
        grid=(t // tm,),
        in_specs=[_pos_spec(tm),
                  pl.BlockSpec((tm, d), lambda i: (i, 0)),
                  pl.BlockSpec((tm, TOP_K), lambda i: (i, 0)),
                  pl.BlockSpec((1, d), lambda i: (0, 0)),
                  pl.BlockSpec(memory_space=pl.ANY)],
        out_specs=pl.BlockSpec((tm, d), lambda i: (i, 0)),
        out_shape=jax.ShapeDtypeStruct((t, d), F32),
        scratch_shapes=[pltpu.VMEM((TOP_K, tm, d), F32), pltpu.SemaphoreType.DMA],
        compiler_params=_params(1, nbytes),
        name="combine",
    )(pos3, x1, gates_t, g_final.reshape(1, d), ys)


def _tiles(t):
    return dict(tm_front=min(256, t), tr=min(2048, t), tm_scatter=min(1024, t), tm_combine=min(512, t),
                bm=min(256, TOP_K * t))


def _tile_pos(pos, tm):
    t = pos.shape[1]
    return pos.reshape(TOP_K, t // tm, tm).transpose(1, 0, 2).reshape(t // tm, 1, TOP_K * tm)


def _router_params(w_group, b_group, w_router, b_router):
    d = w_group.shape[0]
    gpad = SUBLANES - N_GROUPS
    wr = jnp.concatenate([w_group, jnp.zeros((d, gpad), F32), w_router,
                          jnp.zeros((d, LANES - ROUTER_ROWS), F32)], axis=1)
    br = jnp.concatenate([b_group, jnp.zeros((gpad,), F32), b_router,
                          jnp.zeros((LANES - ROUTER_ROWS,), F32)]).reshape(1, LANES)
    return wr, br


def _moe(x1, hn, logits, w1, w3, w2, layer, tl):
    t, d = x1.shape
    meta, gates, cnt = _route(logits, tr=tl["tr"])

    counts = cnt[:, 0].astype(I32)
    start = jnp.cumsum(counts) - counts
    is_e = meta[0:TOP_K, :, None] == jnp.arange(N_EXPERTS, dtype=I32)
    pos = jnp.sum(jnp.where(is_e, start, 0), axis=-1) + meta[TOP_K:2 * TOP_K]
    xs = _scatter_rows(hn, _tile_pos(pos, tl["tm_scatter"]), tm=tl["tm_scatter"])
    tables = _visit_tables(counts, TOP_K * t, tl["bm"])
    ys = _experts(xs, w1, w3, w2, tables, layer=layer, bm=tl["bm"])
    return x1, gates[0:TOP_K].T, pos, ys


def kernel(x, ln_mix, ln_ffn, ln_final, ev_w_in, ev_conv_w, ev_sgu_ln_g, ev_sgu_ln_b, ev_sgu_w, ev_sgu_b, ev_w_out, od_w_in, od_pool_w, od_pool_scale, od_dw_w, od_dw_b, od_ln_g, od_ln_b, od_w_out, moe_w_group, moe_b_group, moe_w_router, moe_b_router, moe_w1, moe_w3, moe_w2):
    bsz, s, d = x.shape
    assert bsz == 1, "token mixers carry state across row tiles of one sequence"
    depth = ln_mix.shape[0]
    tl = _tiles(s)
    tm = tl["tm_front"]
    x_in = x.reshape(s, d)
    for layer in range(depth):
        i = layer // 2
        if layer > 0:
            x1, gates_t, pos, ys = x_in
            x_in = (x1, gates_t, _tile_pos(pos, tm), ys)
        wr, br = _router_params(moe_w_group[layer], moe_b_group[layer],
                                moe_w_router[layer], moe_b_router[layer])
        if layer % 2 == 0:
            c = ev_conv_w.shape[2]
            mix_params = [ev_conv_w[i], ev_sgu_ln_g[i].reshape(1, c), ev_sgu_ln_b[i].reshape(1, c),
                          ev_sgu_w[i], ev_sgu_b[i].T]
            mix_scratch = [pltpu.VMEM((tm + EVEN_HALO, c), F32)]
            x1, hn, logits = _layer_front(
                "even_layer", _even_mix, x_in, ln_mix[layer], ev_w_in[i].astype(BF16), mix_params,
                mix_scratch, ev_w_out[i].astype(BF16), ln_ffn[layer], wr, br, tm=tm)
        else:
            c = od_dw_w.shape[2]
            mix_params = [od_pool_w[i], od_pool_scale[i].reshape(1, c), od_dw_w[i],
                          od_dw_b[i].reshape(1, c), od_ln_g[i].reshape(1, c), od_ln_b[i].reshape(1, c)]
            gd = c // len(POOL_WINDOWS)
            mix_scratch = [pltpu.VMEM((tm + POOL_HALO, c), F32),
                           pltpu.VMEM((tm + POOL_HALO, gd), F32), pltpu.VMEM((tm + POOL_HALO, gd), F32),
                           pltpu.VMEM((tm + CONF_HALO, c), F32),
                           pltpu.VMEM((SUBLANES - 1, tm + CONF_HALO, LANES), F32),
                           pltpu.VMEM((tm, c), F32)]
            x1, hn, logits = _layer_front(
                "odd_layer", _odd_mix, x_in, ln_mix[layer], od_w_in[i].astype(BF16), mix_params,
                mix_scratch, od_w_out[i].astype(BF16), ln_ffn[layer], wr, br, tm=tm)
        x_in = _moe(x1, hn, logits, moe_w1, moe_w3, moe_w2, layer, tl)
    x1, gates_t, pos, ys = x_in
    out = _combine(x1, ys, _tile_pos(pos, tl["tm_combine"]), gates_t, ln_final, tm=tl["tm_combine"])
    return out.reshape(bsz, s, d)
```

```python
import functools
import math

import jax
import jax.numpy as jnp
from jax import lax
from jax.experimental import pallas as pl
from jax.experimental.pallas import tpu as pltpu

F32 = jnp.float32
BF16 = jnp.bfloat16
I32 = jnp.int32

EPS = 1e-6
CHUNK = 64
SGU_BLOCK = 128
SGU_HEADS = 8
POOL_WINDOWS = (2, 4, 8, 16)
N_GROUPS = 4
EXPERTS_PER_GROUP = 8
N_EXPERTS = N_GROUPS * EXPERTS_PER_GROUP
TOP_K = 2

LANES = 128
SUBLANES = 8
VMEM_CAP_BYTES = 60 * 1024 * 1024

ROUTER_ROWS = SUBLANES + N_EXPERTS
NEG_BIG = -1e30


def _vmem_limit(nbytes):
    return int(min(VMEM_CAP_BYTES, nbytes * 5 // 4 + (6 << 20)))


def _params(n_axes, nbytes):
    return pltpu.CompilerParams(
        dimension_semantics=("arbitrary",) * n_axes,
        vmem_limit_bytes=_vmem_limit(nbytes))


def _const_spec(shape, single_buffer=False):
    mode = dict(pipeline_mode=pl.Buffered(1)) if single_buffer else {}
    return pl.BlockSpec(shape, lambda i: (0,) * len(shape), **mode)


def _sigmoid(x):
    return 1.0 / (1.0 + jnp.exp(-x))


def _gelu_tanh(x):
    c = math.sqrt(2.0 / math.pi)
    return x * (0.5 * (1.0 + jnp.tanh(c * (x + 0.044715 * (x * x * x)))))


def _layer_norm(v, g, b):
    mu = jnp.mean(v, axis=-1, keepdims=True)
    vc = v - mu
    var = jnp.mean(vc * vc, axis=-1, keepdims=True)
    return vc * lax.rsqrt(var + EPS) * g + b


def _rms_norm(x, g):
    ms = jnp.mean(x * x, axis=-1, keepdims=True)
    return x * lax.rsqrt(ms + EPS) * g


def _split_bf16(v):
    head = v.astype(BF16)
    return head, (v - head.astype(F32)).astype(BF16)


IN_PROJ_COLS = 1024


def _in_proj(x_ref, g_ref, w_ref, z_ref):
    h = _rms_norm(x_ref[...], g_ref[...]).astype(BF16)
    for c0 in range(0, w_ref.shape[1], IN_PROJ_COLS):
        cols = slice(c0, c0 + IN_PROJ_COLS)
        z_ref[:, cols] = jnp.dot(h, w_ref[:, cols], preferred_element_type=F32)


def _out_proj_route(x_ref, y_ref, w_ref, g_ref, wh_ref, wl_ref, br_ref, x1_ref, hn_ref, lg_ref):
    x1 = x_ref[...] + jnp.dot(y_ref[...], w_ref[...], preferred_element_type=F32)
    x1_ref[...] = x1
    hn = _rms_norm(x1, g_ref[...])
    hn_ref[...] = hn
    head, rem = _split_bf16(hn)
    lg_ref[...] = (jnp.dot(head, wh_ref[...], preferred_element_type=F32)
                   + jnp.dot(rem, wh_ref[...], preferred_element_type=F32)
                   + jnp.dot(head, wl_ref[...], preferred_element_type=F32)
                   + br_ref[...])


EVEN_HALO = SUBLANES


def _even_mix(z_ref, cw_ref, lg_ref, lb_ref, sw_ref, sb_ref, o_ref, pbuf):
    tm = z_ref.shape[0]
    kw, c = cw_ref.shape
    col = lambda j: z_ref[:, j * c:(j + 1) * c]

    @pl.when(pl.program_id(0) == 0)
    def _():
        pbuf[0:EVEN_HALO, :] = jnp.zeros((EVEN_HALO, c), F32)

    p = col(1) * col(2)
    pbuf[EVEN_HALO:EVEN_HALO + tm, :] = p
    conv = cw_ref[kw - 1:kw, :] * p
    for k in range(kw - 1):
        conv = conv + cw_ref[k:k + 1, :] * pbuf[pl.ds(EVEN_HALO - (kw - 1) + k, tm), :]
    o_ref[:, 0:c] = (col(0) * conv).astype(BF16)
    pbuf[0:EVEN_HALO, :] = pbuf[tm:tm + EVEN_HALO, :]

    u = _gelu_tanh(col(3))
    vn = _layer_norm(_gelu_tanh(col(4)), lg_ref[...], lb_ref[...]).astype(BF16)
    tpos = lax.broadcasted_iota(I32, (SGU_BLOCK, SGU_BLOCK), 0)
    spos = lax.broadcasted_iota(I32, (SGU_BLOCK, SGU_BLOCK), 1)
    mask = (spos // CHUNK) <= (tpos // CHUNK)
    hd = c // SGU_HEADS
    for h in range(SGU_HEADS):
        wh = jnp.where(mask, sw_ref[h], 0.0).astype(BF16)
        bh = sb_ref[:, h:h + 1]
        cols = slice(h * hd, (h + 1) * hd)
        for n in range(tm // SGU_BLOCK):
            rows = slice(n * SGU_BLOCK, (n + 1) * SGU_BLOCK)
            sv = jnp.dot(wh, vn[rows, cols], preferred_element_type=F32) + bh
            o_ref[rows, c + h * hd:c + (h + 1) * hd] = (u[rows, cols] * sv).astype(BF16)


POOL_HALO = 24
CONF_HALO = 32
CONV_ROWS = 128


def _window_sum(xbuf, cols, w, lv_a, lv_b, tm):
    lo, hi = SUBLANES, POOL_HALO + tm
    src, src_cols, n = xbuf, cols, 1
    dst, other = lv_a, lv_b
    while 2 * n < w:
        dst[lo:hi, :] = src[lo:hi, src_cols] + src[pl.ds(lo - n, hi - lo), src_cols]
        src, src_cols, n = dst, slice(None), 2 * n
        dst, other = other, dst
    return src[POOL_HALO:hi, src_cols] + src[pl.ds(POOL_HALO - n, tm), src_cols]


def _odd_mix(z_ref, pw_ref, ps_ref, dw_ref, db_ref, lg_ref, lb_ref, o_ref,
             xbuf, lv_a, lv_b, gbuf, sbuf, dbuf):
    tm = z_ref.shape[0]
    kw, c = dw_ref.shape
    i = pl.program_id(0)

    @pl.when(i == 0)
    def _():
        xbuf[0:POOL_HALO, :] = jnp.zeros((POOL_HALO, c), F32)
        gbuf[0:CONF_HALO, :] = jnp.zeros((CONF_HALO, c), F32)
        lv_a[0:SUBLANES, :] = jnp.zeros((SUBLANES, lv_a.shape[1]), F32)
        lv_b[0:SUBLANES, :] = jnp.zeros((SUBLANES, lv_b.shape[1]), F32)

    x = z_ref[:, 0:c]
    xbuf[POOL_HALO:POOL_HALO + tm, :] = x
    step = i * tm + lax.broadcasted_iota(I32, (tm, 1), 0) + 1
    gd = c // len(POOL_WINDOWS)
    for g, w in enumerate(POOL_WINDOWS):
        cols = slice(g * gd, (g + 1) * gd)
        s = _window_sum(xbuf, cols, w, lv_a, lv_b, tm)
        inv_count = 1.0 / jnp.minimum(step, w).astype(F32)
        d = s * inv_count - x[:, cols]
        y = jnp.dot(d.astype(BF16), pw_ref[g].astype(BF16), preferred_element_type=F32)
        o_ref[:, cols] = (y * ps_ref[:, cols]).astype(BF16)
    xbuf[0:POOL_HALO, :] = xbuf[tm:tm + POOL_HALO, :]

    gbuf[CONF_HALO:CONF_HALO + tm, :] = z_ref[:, c:2 * c] * _sigmoid(z_ref[:, 2 * c:3 * c])
    n_rows = CONF_HALO + tm

    def conv_cols(cb, carry):
        lanes = pl.ds(pl.multiple_of(cb * LANES, LANES), LANES)
        for b in range(1, SUBLANES):
            sbuf[b - 1, SUBLANES:n_rows, :] = gbuf[pl.ds(SUBLANES - b, n_rows - SUBLANES), lanes]
        accs = [jnp.broadcast_to(db_ref[:, lanes], (CONV_ROWS, LANES)) for _ in range(0, tm, CONV_ROWS)]
        for back in range(kw):
            a, b = divmod(back, SUBLANES)
            wk = dw_ref[kw - 1 - back:kw - back, lanes]
            for n, r0 in enumerate(range(0, tm, CONV_ROWS)):
                start = r0 + CONF_HALO - SUBLANES * a
                tap = gbuf[start:start + CONV_ROWS, lanes] if b == 0 else sbuf[b - 1, start:start + CONV_ROWS, :]
                accs[n] = accs[n] + wk * tap
        for n, r0 in enumerate(range(0, tm, CONV_ROWS)):
            dbuf[r0:r0 + CONV_ROWS, lanes] = accs[n]
        return carry

    lax.fori_loop(0, c // LANES, conv_cols, 0)
    gbuf[0:CONF_HALO, :] = gbuf[tm:tm + CONF_HALO, :]
    yn = _layer_norm(dbuf[...], lg_ref[...], lb_ref[...])
    o_ref[:, c:2 * c] = (yn * _sigmoid(yn)).astype(BF16)


def _start_row_gathers(pos_ref, ys_ref, ybuf, sem, slot):
    tm = ybuf.shape[2]
    for r in range(tm):
        for k in range(TOP_K):
            p = pos_ref[0, 0, k * tm + r]
            pltpu.make_async_copy(ys_ref.at[pl.ds(p, 1), :], ybuf.at[slot, k, pl.ds(r, 1), :],
                                  sem.at[slot]).start(priority=k)


def _wait_row_gathers(ys_ref, ybuf, sem, slot):
    tm = ybuf.shape[2]
    for k in range(TOP_K):
        _wait_rows(ys_ref.at[pl.ds(0, tm), :], ybuf.at[slot, k], sem.at[slot])


def _gathered_input(pos_cur, pos_next, x1_ref, gt_ref, ys_ref, x_ref, ybuf, sem):
    i = pl.program_id(0)
    slot = i % 2

    @pl.when(i == 0)
    def _():
        _start_row_gathers(pos_cur, ys_ref, ybuf, sem, 0)

    _wait_row_gathers(ys_ref, ybuf, sem, slot)
    x_ref[...] = x1_ref[...] + gt_ref[:, 0:1] * ybuf[slot, 0] + gt_ref[:, 1:2] * ybuf[slot, 1]
    _start_row_gathers(pos_next, ys_ref, ybuf, sem, 1 - slot)


def _front_body(*refs, mix, n_mix_params, gathered):
    n_head = 5 if gathered else 1
    head, refs = refs[:n_head], refs[n_head:]
    gm_ref, win_ref = refs[0:2]
    mix_params = refs[2:2 + n_mix_params]
    wout_ref, gf_ref, wh_ref, wl_ref, br_ref, x1_ref, hn_ref, lgt_ref, z_ref, y_ref = (
        refs[2 + n_mix_params:12 + n_mix_params])
    scratch = refs[12 + n_mix_params:]
    if gathered:
        mix_scratch, (x_ref, ybuf, sem) = scratch[:-3], scratch[-3:]
        _gathered_input(*head, x_ref, ybuf, sem)
    else:
        mix_scratch, x_ref = scratch, head[0]
    _in_proj(x_ref, gm_ref, win_ref, z_ref)
    mix(z_ref, *mix_params, y_ref, *mix_scratch)
    _out_proj_route(x_ref, y_ref, wout_ref, gf_ref, wh_ref, wl_ref, br_ref, x1_ref, hn_ref, lgt_ref)
    if gathered:
        @pl.when(pl.program_id(0) == pl.num_programs(0) - 1)
        def _():
            _wait_row_gathers(head[4], ybuf, sem, 1 - pl.program_id(0) % 2)


def _layer_front(name, mix, x_in, g_mix, w_in, mix_params, mix_scratch, w_out, g_ffn, wr, br, *, tm):
    gathered = isinstance(x_in, tuple)
    t, d = (x_in[0] if gathered else x_in).shape
    nt = t // tm
    n = w_in.shape[1]
    c2 = w_out.shape[0]
    wh, wl = _split_bf16(wr)
    row = lambda width: pl.BlockSpec((tm, width), lambda i: (i, 0))
    tail = [g_ffn.reshape(1, d), wh, wl, br]
    scratch = [pltpu.VMEM((tm, n), F32), pltpu.VMEM((tm, c2), BF16)] + mix_scratch
    if gathered:
        x1, gates_t, pos3, ys = x_in
        head = [pos3, pos3, x1, gates_t, ys]
        head_specs = [_pos_spec(tm),
                      pl.BlockSpec((1, 1, TOP_K * tm), lambda i: (jnp.minimum(i + 1, nt - 1), 0, 0),
                                   memory_space=pltpu.SMEM),
                      row(d), pl.BlockSpec((tm, TOP_K), lambda i: (i, 0)),
                      pl.BlockSpec(memory_space=pl.ANY)]
        scratch += [pltpu.VMEM((tm, d), F32), pltpu.VMEM((2, TOP_K, tm, d), F32),
                    pltpu.SemaphoreType.DMA((2,))]
    else:
        head, head_specs = [x_in], [row(d)]
    nbytes = (d * n * 2 + c2 * d * 2
              + 2 * 3 * tm * d * 4 + 2 * tm * LANES * 4
              + tm * n * 4 + tm * c2 * 2
              + sum(math.prod(s.shape) * 4 for s in mix_scratch)
              + 2 * sum(p.size * p.dtype.itemsize for p in mix_params + tail[1:])
              + (2 * TOP_K + 1) * tm * d * 4 * gathered
              + 6 * tm * d * 4)
    return pl.pallas_call(
        functools.partial(_front_body, mix=mix, n_mix_params=len(mix_params), gathered=gathered),
        grid=(nt,),
        in_specs=(head_specs + [_const_spec((1, d)), _const_spec(w_in.shape, single_buffer=True)]
                  + [_const_spec(p.shape) for p in mix_params]
                  + [_const_spec(w_out.shape, single_buffer=True), _const_spec((1, d)),
                     _const_spec(wh.shape), _const_spec(wl.shape), _const_spec(br.shape)]),
        out_specs=[row(d), row(d), row(LANES)],
        out_shape=[jax.ShapeDtypeStruct((t, d), F32),
                   jax.ShapeDtypeStruct((t, d), F32),
                   jax.ShapeDtypeStruct((t, LANES), F32)],
        scratch_shapes=scratch,
        compiler_params=_params(1, nbytes),
        name=name,
    )(*head, g_mix.reshape(1, d), w_in, *mix_params, w_out, *tail)


RANK_BLOCK = 256


def _route_block(logits, run, before):
    n = logits.shape[1]
    row8 = lax.broadcasted_iota(I32, (SUBLANES, n), 0)
    row = lax.broadcasted_iota(I32, (N_EXPERTS, n), 0)

    gl = jnp.where(row8 < N_GROUPS, logits[0:SUBLANES], NEG_BIG)
    gmax = jnp.max(gl, axis=0, keepdims=True)
    gidx = jnp.min(jnp.where(gl == gmax, row8, N_EXPERTS), axis=0, keepdims=True)
    p_group = 1.0 / jnp.sum(jnp.exp(gl - gmax), axis=0, keepdims=True)

    el = jnp.where(row // EXPERTS_PER_GROUP == gidx, logits[SUBLANES:ROUTER_ROWS], NEG_BIG)
    m0 = jnp.max(el, axis=0, keepdims=True)
    e0 = jnp.min(jnp.where(el == m0, row, N_EXPERTS), axis=0, keepdims=True)
    el1 = jnp.where(row == e0, NEG_BIG, el)
    m1 = jnp.max(el1, axis=0, keepdims=True)
    e1 = jnp.min(jnp.where(el1 == m1, row, N_EXPERTS), axis=0, keepdims=True)
    ratio = jnp.exp(m1 - m0)
    gate0 = p_group / (1.0 + ratio)
    gate1 = p_group * ratio / (1.0 + ratio)

    sel0 = row == e0
    sel1 = row == e1
    pre0 = jnp.dot(sel0.astype(BF16), before, preferred_element_type=F32)
    pre1 = jnp.dot(sel1.astype(BF16), before, preferred_element_type=F32)
    cnt0 = jnp.sum(sel0.astype(F32), axis=1, keepdims=True)
    cnt1 = jnp.sum(sel1.astype(F32), axis=1, keepdims=True)
    rank0 = jnp.sum(jnp.where(sel0, pre0 + run, 0.0), axis=0, keepdims=True)
    rank1 = jnp.sum(jnp.where(sel1, pre1 + (run + cnt0), 0.0), axis=0, keepdims=True)

    meta = jnp.where(row8 == 0, e0, jnp.where(row8 == 1, e1, jnp.where(
        row8 == 2, rank0.astype(I32), jnp.where(row8 == 3, rank1.astype(I32), 0))))
    gates = jnp.where(row8 == 0, gate0, jnp.where(row8 == 1, gate1, 0.0))
    return meta, gates, run + (cnt0 + cnt1)


def _route_body(lg_ref, meta_ref, gate_ref, cnt_ref, run_ref):
    tr = lg_ref.shape[0]
    nb = min(RANK_BLOCK, tr)

    @pl.when(pl.program_id(0) == 0)
    def _():
        run_ref[...] = jnp.zeros(run_ref.shape, F32)

    src = lax.broadcasted_iota(I32, (nb, nb), 0)
    dst = lax.broadcasted_iota(I32, (nb, nb), 1)
    before = (src < dst).astype(BF16)
    run = run_ref[:, 0:1]
    for b in range(tr // nb):
        cols = slice(b * nb, (b + 1) * nb)
        logits_t = lg_ref[cols, :].T[0:ROUTER_ROWS]
        meta, gates, run = _route_block(logits_t, run, before)
        meta_ref[:, cols] = meta
        gate_ref[:, cols] = gates
    run_ref[...] = jnp.broadcast_to(run, run_ref.shape)
    cnt_ref[...] = run_ref[...]


def _route(logits, *, tr):
    t = logits.shape[0]
    return pl.pallas_call(
        _route_body,
        grid=(t // tr,),
        in_specs=[pl.BlockSpec((tr, LANES), lambda i: (i, 0))],
        out_specs=[pl.BlockSpec((SUBLANES, tr), lambda i: (0, i)),
                   pl.BlockSpec((SUBLANES, tr), lambda i: (0, i)),
                   pl.BlockSpec((N_EXPERTS, LANES), lambda i: (0, 0))],
        out_shape=[jax.ShapeDtypeStruct((SUBLANES, t), I32),
                   jax.ShapeDtypeStruct((SUBLANES, t), F32),
                   jax.ShapeDtypeStruct((N_EXPERTS, LANES), F32)],
        scratch_shapes=[pltpu.VMEM((N_EXPERTS, LANES), F32)],
        compiler_params=_params(1, 64 * tr * 4 * 4),
        name="route",
    )(logits)


def _pos_spec(tm):
    return pl.BlockSpec((1, 1, TOP_K * tm), lambda i: (i, 0, 0), memory_space=pltpu.SMEM)


def _wait_rows(src_ref, dst_ref, sem):
    pltpu.make_async_copy(src_ref, dst_ref, sem).wait()


def _scatter_body(pos_ref, h_ref, xs_ref, sem):
    tm = h_ref.shape[0]

    for r in range(tm):
        for k in range(TOP_K):
            p = pos_ref[0, 0, k * tm + r]
            pltpu.make_async_copy(h_ref.at[pl.ds(r, 1), :], xs_ref.at[pl.ds(p, 1), :],
                                  sem).start(priority=k)
    for k in range(TOP_K):
        _wait_rows(h_ref, xs_ref.at[pl.ds(0, tm), :], sem)


def _scatter_rows(hn, pos3, *, tm):
    t, d = hn.shape
    return pl.pallas_call(
        _scatter_body,
        grid=(t // tm,),
        in_specs=[_pos_spec(tm), pl.BlockSpec((tm, d), lambda i: (i, 0))],
        out_specs=pl.BlockSpec(memory_space=pl.ANY),
        out_shape=jax.ShapeDtypeStruct((TOP_K * t, d), hn.dtype),
        scratch_shapes=[pltpu.SemaphoreType.DMA],
        compiler_params=_params(1, 2 * tm * d * 4),
        name="scatter_rows",
    )(pos3, hn)


def _expert_body(vblk, vexp, vfirst, vnew, vvalid, vslot, vnext, vhalf, estart,
                 x_ref, w1_hbm, w3_hbm, w2_hbm, o_ref, wf1, wf3, wf2, w1b, w3b, w2b, sem, *, layer):
    bm = x_ref.shape[0]
    v = pl.program_id(0)
    weights = ((w1_hbm, wf1), (w3_hbm, wf3), (w2_hbm, wf2))

    def weight_copies(e, slot):
        return [pltpu.make_async_copy(w_hbm.at[layer, e], wf.at[slot], sem.at[slot, j])
                for j, (w_hbm, wf) in enumerate(weights)]

    @pl.when(v == 0)
    def _():
        for c in weight_copies(vexp[0], 0):
            c.start()

    @pl.when(vvalid[v] == 1)
    def _():
        @pl.when(vnew[v] == 1)
        def _():
            slot = vslot[v]
            for c in weight_copies(vexp[v], slot):
                c.wait()

            @pl.when(vnext[v] >= 0)
            def _():
                for c in weight_copies(vnext[v], 1 - slot):
                    c.start()

            w1b[...] = wf1[slot].astype(BF16)
            w3b[...] = wf3[slot].astype(BF16)
            w2b[...] = wf2[slot].astype(BF16)

        def visit(r0, n, r0_other, n_other):
            rows = slice(r0, r0 + n)
            xb = x_ref[rows, :].astype(BF16)
            h1 = jnp.dot(xb, w1b[...], preferred_element_type=F32)
            h3 = jnp.dot(xb, w3b[...], preferred_element_type=F32)
            hid = (h1 * _sigmoid(h1)) * h3
            y = jnp.dot(hid.astype(BF16), w2b[...], preferred_element_type=F32)

            @pl.when(vfirst[v] == 1)
            def _():
                o_ref[rows, :] = y
                if n_other:
                    o_ref[r0_other:r0_other + n_other, :] = jnp.zeros((n_other, o_ref.shape[1]), F32)

            @pl.when(vfirst[v] == 0)
            def _():
                e = vexp[v]
                r = vblk[v] * bm + r0 + lax.broadcasted_iota(I32, (n, 1), 0)
                mine = (r >= estart[e]) & (r < estart[e + 1])
                o_ref[rows, :] = jnp.where(mine, y, o_ref[rows, :])

        hb = bm // 2

        @pl.when(vhalf[v] == 0)
        def _():
            visit(0, bm, 0, 0)

        @pl.when(vhalf[v] == 1)
        def _():
            visit(0, hb, hb, bm - hb)

        @pl.when(vhalf[v] == 2)
        def _():
            visit(hb, bm - hb, 0, hb)


def _experts(xs, w1, w3, w2, tables, *, layer, bm):
    n, d = xs.shape
    f = w1.shape[3]
    n_visits = tables[0].shape[0]
    nbytes = 2 * (bm * d * 4 * 2 + 3 * d * f * 4) + 3 * d * f * 2 + 6 * bm * d * 4
    hbm = pl.BlockSpec(memory_space=pl.ANY)
    grid_spec = pltpu.PrefetchScalarGridSpec(
        num_scalar_prefetch=len(tables),
        grid=(n_visits,),
        in_specs=[pl.BlockSpec((bm, d), lambda v, blk, *_: (blk[v], 0)), hbm, hbm, hbm],
        out_specs=pl.BlockSpec((bm, d), lambda v, blk, *_: (blk[v], 0)),
        scratch_shapes=[pltpu.VMEM((2, d, f), F32), pltpu.VMEM((2, d, f), F32), pltpu.VMEM((2, f, d), F32),
                        pltpu.VMEM((d, f), BF16), pltpu.VMEM((d, f), BF16), pltpu.VMEM((f, d), BF16),
                        pltpu.SemaphoreType.DMA((2, 3))])
    return pl.pallas_call(
        functools.partial(_expert_body, layer=layer),
        grid_spec=grid_spec,
        out_shape=jax.ShapeDtypeStruct((n, d), F32),
        compiler_params=_params(1, nbytes),
        name="experts",
    )(*tables, xs, w1, w3, w2)


def _visit_tables(counts, n_slots, bm):
    n_blocks = n_slots // bm
    n_visits = n_blocks + N_EXPERTS - 1
    end = jnp.cumsum(counts)
    start = end - counts
    first_blk = start // bm
    last_blk = jnp.maximum(end - 1, 0) // bm
    per_e = jnp.where(counts > 0, last_blk - first_blk + 1, 0)
    cum = jnp.cumsum(per_e)
    total = cum[-1]
    v = jnp.minimum(jnp.arange(n_visits, dtype=I32), total - 1)
    e = jnp.sum((cum[None, :] <= v[:, None]).astype(I32), axis=1)
    blk = (first_blk[e] + (v - (cum[e] - per_e[e]))).astype(I32)
    valid = (jnp.arange(n_visits) < total).astype(I32)
    prev_blk = jnp.concatenate([jnp.full((1,), -1, I32), blk[:-1]])
    prev_e = jnp.concatenate([jnp.full((1,), -1, I32), e[:-1]])
    first = (blk != prev_blk).astype(I32)
    new = (e != prev_e).astype(I32)
    ordinal = jnp.cumsum(new) - 1
    slot = (ordinal % 2).astype(I32)
    is_next = (new[None, :] == 1) & (ordinal[None, :] == ordinal[:, None] + 1)
    nxt = jnp.where(jnp.any(is_next, axis=1), jnp.sum(jnp.where(is_next, e[None, :], 0), axis=1), -1)
    lo = jnp.maximum(start[e], blk * bm) - blk * bm
    hi = jnp.minimum(end[e], (blk + 1) * bm) - blk * bm
    half = jnp.where(hi <= bm // 2, 1, jnp.where(lo >= bm // 2, 2, 0)).astype(I32)
    estart = jnp.concatenate([start, end[-1:]]).astype(I32)
    return blk, e, first, new, valid, slot, nxt.astype(I32), half, estart


def _combine_body(pos_ref, x_ref, gt_ref, g_ref, ys_ref, o_ref, ybuf, sem):
    tm = x_ref.shape[0]
    for r in range(tm):
        for k in range(TOP_K):
            p = pos_ref[0, 0, k * tm + r]
            pltpu.make_async_copy(ys_ref.at[pl.ds(p, 1), :], ybuf.at[k, pl.ds(r, 1), :],
                                  sem).start(priority=k)
    for k in range(TOP_K):
        _wait_rows(ys_ref.at[pl.ds(0, tm), :], ybuf.at[k], sem)
    x2 = x_ref[...] + gt_ref[:, 0:1] * ybuf[0] + gt_ref[:, 1:2] * ybuf[1]
    o_ref[...] = _rms_norm(x2, g_ref[...])


def _combine(x1, ys, pos3, gates_t, g_final, *, tm):
    t, d = x1.shape
    nbytes = 2 * (2 * tm * d * 4 + tm * LANES * 4) + TOP_K * tm * d * 4 + 2 * tm * d * 4
    return pl.pallas_call(
        _combine_body,
        grid=(t // tm,),
        in_specs=[_pos_spec(tm),
                  pl.BlockSpec((tm, d), lambda i: (i, 0)),
                  pl.BlockSpec((tm, TOP_K), lambda i: (i, 0)),
                  pl.BlockSpec((1, d), lambda i: (0, 0)),
                  pl.BlockSpec(memory_space=pl.ANY)],
        out_specs=pl.BlockSpec((tm, d), lambda i: (i, 0)),
        out_shape=jax.ShapeDtypeStruct((t, d), F32),
        scratch_shapes=[pltpu.VMEM((TOP_K, tm, d), F32), pltpu.SemaphoreType.DMA],
        compiler_params=_params(1, nbytes),
        name="combine",
    )(pos3, x1, gates_t, g_final.reshape(1, d), ys)


def _tiles(t):
    return dict(tm_front=min(256, t), tr=min(2048, t), tm_scatter=min(2048, t), tm_combine=min(1024, t),
                bm=min(256, TOP_K * t))


def _tile_pos(pos, tm):
    t = pos.shape[1]
    return pos.reshape(TOP_K, t // tm, tm).transpose(1, 0, 2).reshape(t // tm, 1, TOP_K * tm)


def _router_params(w_group, b_group, w_router, b_router):
    d = w_group.shape[0]
    gpad = SUBLANES - N_GROUPS
    wr = jnp.concatenate([w_group, jnp.zeros((d, gpad), F32), w_router,
                          jnp.zeros((d, LANES - ROUTER_ROWS), F32)], axis=1)
    br = jnp.concatenate([b_group, jnp.zeros((gpad,), F32), b_router,
                          jnp.zeros((LANES - ROUTER_ROWS,), F32)]).reshape(1, LANES)
    return wr, br


def _moe(x1, hn, logits, w1, w3, w2, layer, tl):
    t, d = x1.shape
    meta, gates, cnt = _route(logits, tr=tl["tr"])

    counts = cnt[:, 0].astype(I32)
    start = jnp.cumsum(counts) - counts
    is_e = meta[0:TOP_K, :, None] == jnp.arange(N_EXPERTS, dtype=I32)
    pos = jnp.sum(jnp.where(is_e, start, 0), axis=-1) + meta[TOP_K:2 * TOP_K]
    xs = _scatter_rows(hn, _tile_pos(pos, tl["tm_scatter"]), tm=tl["tm_scatter"])
    tables = _visit_tables(counts, TOP_K * t, tl["bm"])
    ys = _experts(xs, w1, w3, w2, tables, layer=layer, bm=tl["bm"])
    return x1, gates[0:TOP_K].T, pos, ys


def kernel(x, ln_mix, ln_ffn, ln_final, ev_w_in, ev_conv_w, ev_sgu_ln_g, ev_sgu_ln_b, ev_sgu_w, ev_sgu_b, ev_w_out, od_w_in, od_pool_w, od_pool_scale, od_dw_w, od_dw_b, od_ln_g, od_ln_b, od_w_out, moe_w_group, moe_b_group, moe_w_router, moe_b_router, moe_w1, moe_w3, moe_w2):
    bsz, s, d = x.shape
    assert bsz == 1, "token mixers carry state across row tiles of one sequence"
    depth = ln_mix.shape[0]
    tl = _tiles(s)
    tm = tl["tm_front"]
    x_in = x.reshape(s, d)
    for layer in range(depth):
        i = layer // 2
        if layer > 0:
            x1, gates_t, pos, ys = x_in
            x_in = (x1, gates_t, _tile_pos(pos, tm), ys)
        wr, br = _router_params(moe_w_group[layer], moe_b_group[layer],
                                moe_w_router[layer], moe_b_router[layer])
        if layer % 2 == 0:
            c = ev_conv_w.shape[2]
            mix_params = [ev_conv_w[i], ev_sgu_ln_g[i].reshape(1, c), ev_sgu_ln_b[i].reshape(1, c),
                          ev_sgu_w[i], ev_sgu_b[i].T]
            mix_scratch = [pltpu.VMEM((tm + EVEN_HALO, c), F32)]
            x1, hn, logits = _layer_front(
                "even_layer", _even_mix, x_in, ln_mix[layer], ev_w_in[i].astype(BF16), mix_params,
                mix_scratch, ev_w_out[i].astype(BF16), ln_ffn[layer], wr, br, tm=tm)
        else:
            c = od_dw_w.shape[2]
            mix_params = [od_pool_w[i], od_pool_scale[i].reshape(1, c), od_dw_w[i],
                          od_dw_b[i].reshape(1, c), od_ln_g[i].reshape(1, c), od_ln_b[i].reshape(1, c)]
            gd = c // len(POOL_WINDOWS)
            mix_scratch = [pltpu.VMEM((tm + POOL_HALO, c), F32),
                           pltpu.VMEM((tm + POOL_HALO, gd), F32), pltpu.VMEM((tm + POOL_HALO, gd), F32),
                           pltpu.VMEM((tm + CONF_HALO, c), F32),
                           pltpu.VMEM((SUBLANES - 1, tm + CONF_HALO, LANES), F32),
                           pltpu.VMEM((tm, c), F32)]
            x1, hn, logits = _layer_front(
                "odd_layer", _odd_mix, x_in, ln_mix[layer], od_w_in[i].astype(BF16), mix_params,
                mix_scratch, od_w_out[i].astype(BF16), ln_ffn[layer], wr, br, tm=tm)
        x_in = _moe(x1, hn, logits, moe_w1, moe_w3, moe_w2, layer, tl)
    x1, gates_t, pos, ys = x_in
    out = _combine(x1, ys, _tile_pos(pos, tl["tm_combine"]), gates_t, ln_final, tm=tl["tm_combine"])
    return out.reshape(bsz, s, d)
```

```python
import functools
import math

import jax
import jax.numpy as jnp
from jax import lax
from jax.experimental import pallas as pl
from jax.experimental.pallas import tpu as pltpu

F32 = jnp.float32
BF16 = jnp.bfloat16
I32 = jnp.int32

EPS = 1e-6
CHUNK = 64
SGU_BLOCK = 128
SGU_HEADS = 8
POOL_WINDOWS = (2, 4, 8, 16)
N_GROUPS = 4
EXPERTS_PER_GROUP = 8
N_EXPERTS = N_GROUPS * EXPERTS_PER_GROUP
TOP_K = 2

LANES = 128
SUBLANES = 8
VMEM_CAP_BYTES = 60 * 1024 * 1024

ROUTER_ROWS = SUBLANES + N_EXPERTS
NEG_BIG = -1e30


def _vmem_limit(nbytes):
    return int(min(VMEM_CAP_BYTES, nbytes * 5 // 4 + (6 << 20)))


def _params(n_axes, nbytes):
    return pltpu.CompilerParams(
        dimension_semantics=("arbitrary",) * n_axes,
        vmem_limit_bytes=_vmem_limit(nbytes))


def _const_spec(shape, single_buffer=False):
    mode = dict(pipeline_mode=pl.Buffered(1)) if single_buffer else {}
    return pl.BlockSpec(shape, lambda i: (0,) * len(shape), **mode)


def _sigmoid(x):
    return 1.0 / (1.0 + jnp.exp(-x))


def _gelu_tanh(x):
    c = math.sqrt(2.0 / math.pi)
    return x * (0.5 * (1.0 + jnp.tanh(c * (x + 0.044715 * (x * x * x)))))


def _layer_norm(v, g, b):
    mu = jnp.mean(v, axis=-1, keepdims=True)
    vc = v - mu
    var = jnp.mean(vc * vc, axis=-1, keepdims=True)
    return vc * lax.rsqrt(var + EPS) * g + b


def _rms_norm(x, g):
    ms = jnp.mean(x * x, axis=-1, keepdims=True)
    return x * lax.rsqrt(ms + EPS) * g


def _split_bf16(v):
    head = v.astype(BF16)
    return head, (v - head.astype(F32)).astype(BF16)


IN_PROJ_COLS = 1024


def _in_proj(x_ref, g_ref, w_ref, z_ref):
    h = _rms_norm(x_ref[...], g_ref[...]).astype(BF16)
    for c0 in range(0, w_ref.shape[1], IN_PROJ_COLS):
        cols = slice(c0, c0 + IN_PROJ_COLS)
        z_ref[:, cols] = jnp.dot(h, w_ref[:, cols], preferred_element_type=F32)


def _out_proj_route(x_ref, y_ref, w_ref, g_ref, wh_ref, wl_ref, br_ref, x1_ref, hn_ref, lg_ref):
    x1 = x_ref[...] + jnp.dot(y_ref[...], w_ref[...], preferred_element_type=F32)
    x1_ref[...] = x1
    hn = _rms_norm(x1, g_ref[...])
    hn_ref[...] = hn
    head, rem = _split_bf16(hn)
    lg_ref[...] = (jnp.dot(head, wh_ref[...], preferred_element_type=F32)
                   + jnp.dot(rem, wh_ref[...], preferred_element_type=F32)
                   + jnp.dot(head, wl_ref[...], preferred_element_type=F32)
                   + br_ref[...])


EVEN_HALO = SUBLANES


def _even_mix(z_ref, cw_ref, lg_ref, lb_ref, sw_ref, sb_ref, o_ref, pbuf):
    tm = z_ref.shape[0]
    kw, c = cw_ref.shape
    col = lambda j: z_ref[:, j * c:(j + 1) * c]

    @pl.when(pl.program_id(0) == 0)
    def _():
        pbuf[0:EVEN_HALO, :] = jnp.zeros((EVEN_HALO, c), F32)

    p = col(1) * col(2)
    pbuf[EVEN_HALO:EVEN_HALO + tm, :] = p
    conv = cw_ref[kw - 1:kw, :] * p
    for k in range(kw - 1):
        conv = conv + cw_ref[k:k + 1, :] * pbuf[pl.ds(EVEN_HALO - (kw - 1) + k, tm), :]
    o_ref[:, 0:c] = (col(0) * conv).astype(BF16)
    pbuf[0:EVEN_HALO, :] = pbuf[tm:tm + EVEN_HALO, :]

    u = _gelu_tanh(col(3))
    vn = _layer_norm(_gelu_tanh(col(4)), lg_ref[...], lb_ref[...]).astype(BF16)
    tpos = lax.broadcasted_iota(I32, (SGU_BLOCK, SGU_BLOCK), 0)
    spos = lax.broadcasted_iota(I32, (SGU_BLOCK, SGU_BLOCK), 1)
    mask = (spos // CHUNK) <= (tpos // CHUNK)
    hd = c // SGU_HEADS
    for h in range(SGU_HEADS):
        wh = jnp.where(mask, sw_ref[h], 0.0).astype(BF16)
        bh = sb_ref[:, h:h + 1]
        cols = slice(h * hd, (h + 1) * hd)
        for n in range(tm // SGU_BLOCK):
            rows = slice(n * SGU_BLOCK, (n + 1) * SGU_BLOCK)
            sv = jnp.dot(wh, vn[rows, cols], preferred_element_type=F32) + bh
            o_ref[rows, c + h * hd:c + (h + 1) * hd] = (u[rows, cols] * sv).astype(BF16)


POOL_HALO = 24
CONF_HALO = 32
CONV_ROWS = 128


def _window_sum(xbuf, cols, w, lv_a, lv_b, tm):
    lo, hi = SUBLANES, POOL_HALO + tm
    src, src_cols, n = xbuf, cols, 1
    dst, other = lv_a, lv_b
    while 2 * n < w:
        dst[lo:hi, :] = src[lo:hi, src_cols] + src[pl.ds(lo - n, hi - lo), src_cols]
        src, src_cols, n = dst, slice(None), 2 * n
        dst, other = other, dst
    return src[POOL_HALO:hi, src_cols] + src[pl.ds(POOL_HALO - n, tm), src_cols]


def _odd_mix(z_ref, pw_ref, ps_ref, dw_ref, db_ref, lg_ref, lb_ref, o_ref,
             xbuf, lv_a, lv_b, gbuf, sbuf, dbuf):
    tm = z_ref.shape[0]
    kw, c = dw_ref.shape
    i = pl.program_id(0)

    @pl.when(i == 0)
    def _():
        xbuf[0:POOL_HALO, :] = jnp.zeros((POOL_HALO, c), F32)
        gbuf[0:CONF_HALO, :] = jnp.zeros((CONF_HALO, c), F32)
        lv_a[0:SUBLANES, :] = jnp.zeros((SUBLANES, lv_a.shape[1]), F32)
        lv_b[0:SUBLANES, :] = jnp.zeros((SUBLANES, lv_b.shape[1]), F32)

    x = z_ref[:, 0:c]
    xbuf[POOL_HALO:POOL_HALO + tm, :] = x
    step = i * tm + lax.broadcasted_iota(I32, (tm, 1), 0) + 1
    gd = c // len(POOL_WINDOWS)
    for g, w in enumerate(POOL_WINDOWS):
        cols = slice(g * gd, (g + 1) * gd)
        s = _window_sum(xbuf, cols, w, lv_a, lv_b, tm)
        inv_count = 1.0 / jnp.minimum(step, w).astype(F32)
        d = s * inv_count - x[:, cols]
        y = jnp.dot(d.astype(BF16), pw_ref[g].astype(BF16), preferred_element_type=F32)
        o_ref[:, cols] = (y * ps_ref[:, cols]).astype(BF16)
    xbuf[0:POOL_HALO, :] = xbuf[tm:tm + POOL_HALO, :]

    gbuf[CONF_HALO:CONF_HALO + tm, :] = z_ref[:, c:2 * c] * _sigmoid(z_ref[:, 2 * c:3 * c])
    n_rows = CONF_HALO + tm

    def conv_cols(cb, carry):
        lanes = pl.ds(pl.multiple_of(cb * LANES, LANES), LANES)
        for b in range(1, SUBLANES):
            sbuf[b - 1, SUBLANES:n_rows, :] = gbuf[pl.ds(SUBLANES - b, n_rows - SUBLANES), lanes]
        accs = [jnp.broadcast_to(db_ref[:, lanes], (CONV_ROWS, LANES)) for _ in range(0, tm, CONV_ROWS)]
        for back in range(kw):
            a, b = divmod(back, SUBLANES)
            wk = dw_ref[kw - 1 - back:kw - back, lanes]
            for n, r0 in enumerate(range(0, tm, CONV_ROWS)):
                start = r0 + CONF_HALO - SUBLANES * a
                tap = gbuf[start:start + CONV_ROWS, lanes] if b == 0 else sbuf[b - 1, start:start + CONV_ROWS, :]
                accs[n] = accs[n] + wk * tap
        for n, r0 in enumerate(range(0, tm, CONV_ROWS)):
            dbuf[r0:r0 + CONV_ROWS, lanes] = accs[n]
        return carry

    lax.fori_loop(0, c // LANES, conv_cols, 0)
    gbuf[0:CONF_HALO, :] = gbuf[tm:tm + CONF_HALO, :]
    yn = _layer_norm(dbuf[...], lg_ref[...], lb_ref[...])
    o_ref[:, c:2 * c] = (yn * _sigmoid(yn)).astype(BF16)


def _start_row_gathers(pos_ref, ys_ref, ybuf, sem, slot):
    tm = ybuf.shape[2]
    for r in range(tm):
        for k in range(TOP_K):
            p = pos_ref[0, 0, k * tm + r]
            pltpu.make_async_copy(ys_ref.at[pl.ds(p, 1), :], ybuf.at[slot, k, pl.ds(r, 1), :],
                                  sem.at[slot]).start(priority=k)


def _wait_row_gathers(ys_ref, ybuf, sem, slot):
    tm = ybuf.shape[2]
    for k in range(TOP_K):
        _wait_rows(ys_ref.at[pl.ds(0, tm), :], ybuf.at[slot, k], sem.at[slot])


def _gathered_input(pos_cur, pos_next, x1_ref, gt_ref, ys_ref, x_ref, ybuf, sem):
    i = pl.program_id(0)
    slot = i % 2

    @pl.when(i == 0)
    def _():
        _start_row_gathers(pos_cur, ys_ref, ybuf, sem, 0)

    _wait_row_gathers(ys_ref, ybuf, sem, slot)
    x_ref[...] = x1_ref[...] + gt_ref[:, 0:1] * ybuf[slot, 0] + gt_ref[:, 1:2] * ybuf[slot, 1]
    _start_row_gathers(pos_next, ys_ref, ybuf, sem, 1 - slot)


def _front_body(*refs, mix, n_mix_params, gathered):
    n_head = 5 if gathered else 1
    head, refs = refs[:n_head], refs[n_head:]
    gm_ref, win_ref = refs[0:2]
    mix_params = refs[2:2 + n_mix_params]
    wout_ref, gf_ref, wh_ref, wl_ref, br_ref, x1_ref, hn_ref, lgt_ref, z_ref, y_ref = (
        refs[2 + n_mix_params:12 + n_mix_params])
    scratch = refs[12 + n_mix_params:]
    if gathered:
        mix_scratch, (x_ref, ybuf, sem) = scratch[:-3], scratch[-3:]
        _gathered_input(*head, x_ref, ybuf, sem)
    else:
        mix_scratch, x_ref = scratch, head[0]
    _in_proj(x_ref, gm_ref, win_ref, z_ref)
    mix(z_ref, *mix_params, y_ref, *mix_scratch)
    _out_proj_route(x_ref, y_ref, wout_ref, gf_ref, wh_ref, wl_ref, br_ref, x1_ref, hn_ref, lgt_ref)
    if gathered:
        @pl.when(pl.program_id(0) == pl.num_programs(0) - 1)
        def _():
            _wait_row_gathers(head[4], ybuf, sem, 1 - pl.program_id(0) % 2)


def _layer_front(name, mix, x_in, g_mix, w_in, mix_params, mix_scratch, w_out, g_ffn, wr, br, *, tm):
    gathered = isinstance(x_in, tuple)
    t, d = (x_in[0] if gathered else x_in).shape
    nt = t // tm
    n = w_in.shape[1]
    c2 = w_out.shape[0]
    wh, wl = _split_bf16(wr)
    row = lambda width: pl.BlockSpec((tm, width), lambda i: (i, 0))
    tail = [g_ffn.reshape(1, d), wh, wl, br]
    scratch = [pltpu.VMEM((tm, n), F32), pltpu.VMEM((tm, c2), BF16)] + mix_scratch
    if gathered:
        x1, gates_t, pos3, ys = x_in
        head = [pos3, pos3, x1, gates_t, ys]
        head_specs = [_pos_spec(tm),
                      pl.BlockSpec((1, 1, TOP_K * tm), lambda i: (jnp.minimum(i + 1, nt - 1), 0, 0),
                                   memory_space=pltpu.SMEM),
                      row(d), pl.BlockSpec((tm, TOP_K), lambda i: (i, 0)),
                      pl.BlockSpec(memory_space=pl.ANY)]
        scratch += [pltpu.VMEM((tm, d), F32), pltpu.VMEM((2, TOP_K, tm, d), F32),
                    pltpu.SemaphoreType.DMA((2,))]
    else:
        head, head_specs = [x_in], [row(d)]
    nbytes = (d * n * 2 + c2 * d * 2
              + 2 * 3 * tm * d * 4 + 2 * tm * LANES * 4
              + tm * n * 4 + tm * c2 * 2
              + sum(math.prod(s.shape) * 4 for s in mix_scratch)
              + 2 * sum(p.size * p.dtype.itemsize for p in mix_params + tail[1:])
              + (2 * TOP_K + 1) * tm * d * 4 * gathered
              + 6 * tm * d * 4)
    return pl.pallas_call(
        functools.partial(_front_body, mix=mix, n_mix_params=len(mix_params), gathered=gathered),
        grid=(nt,),
        in_specs=(head_specs + [_const_spec((1, d)), _const_spec(w_in.shape, single_buffer=True)]
                  + [_const_spec(p.shape) for p in mix_params]
                  + [_const_spec(w_out.shape, single_buffer=True), _const_spec((1, d)),
                     _const_spec(wh.shape), _const_spec(wl.shape), _const_spec(br.shape)]),
        out_specs=[row(d), row(d), row(LANES)],
        out_shape=[jax.ShapeDtypeStruct((t, d), F32),
                   jax.ShapeDtypeStruct((t, d), F32),
                   jax.ShapeDtypeStruct((t, LANES), F32)],
        scratch_shapes=scratch,
        compiler_params=_params(1, nbytes),
        name=name,
    )(*head, g_mix.reshape(1, d), w_in, *mix_params, w_out, *tail)


RANK_BLOCK = 256


def _route_block(logits, run, before):
    n = logits.shape[1]
    row8 = lax.broadcasted_iota(I32, (SUBLANES, n), 0)
    row = lax.broadcasted_iota(I32, (N_EXPERTS, n), 0)

    gl = jnp.where(row8 < N_GROUPS, logits[0:SUBLANES], NEG_BIG)
    gmax = jnp.max(gl, axis=0, keepdims=True)
    gidx = jnp.min(jnp.where(gl == gmax, row8, N_EXPERTS), axis=0, keepdims=True)
    p_group = 1.0 / jnp.sum(jnp.exp(gl - gmax), axis=0, keepdims=True)

    el = jnp.where(row // EXPERTS_PER_GROUP == gidx, logits[SUBLANES:ROUTER_ROWS], NEG_BIG)
    m0 = jnp.max(el, axis=0, keepdims=True)
    e0 = jnp.min(jnp.where(el == m0, row, N_EXPERTS), axis=0, keepdims=True)
    el1 = jnp.where(row == e0, NEG_BIG, el)
    m1 = jnp.max(el1, axis=0, keepdims=True)
    e1 = jnp.min(jnp.where(el1 == m1, row, N_EXPERTS), axis=0, keepdims=True)
    ratio = jnp.exp(m1 - m0)
    gate0 = p_group / (1.0 + ratio)
    gate1 = p_group * ratio / (1.0 + ratio)

    sel0 = row == e0
    sel1 = row == e1
    pre0 = jnp.dot(sel0.astype(BF16), before, preferred_element_type=F32)
    pre1 = jnp.dot(sel1.astype(BF16), before, preferred_element_type=F32)
    cnt0 = jnp.sum(sel0.astype(F32), axis=1, keepdims=True)
    cnt1 = jnp.sum(sel1.astype(F32), axis=1, keepdims=True)
    rank0 = jnp.sum(jnp.where(sel0, pre0 + run, 0.0), axis=0, keepdims=True)
    rank1 = jnp.sum(jnp.where(sel1, pre1 + (run + cnt0), 0.0), axis=0, keepdims=True)

    meta = jnp.where(row8 == 0, e0, jnp.where(row8 == 1, e1, jnp.where(
        row8 == 2, rank0.astype(I32), jnp.where(row8 == 3, rank1.astype(I32), 0))))
    gates = jnp.where(row8 == 0, gate0, jnp.where(row8 == 1, gate1, 0.0))
    return meta, gates, run + (cnt0 + cnt1)


def _route_body(lg_ref, meta_ref, gate_ref, cnt_ref, run_ref):
    tr = lg_ref.shape[0]
    nb = min(RANK_BLOCK, tr)

    @pl.when(pl.program_id(0) == 0)
    def _():
        run_ref[...] = jnp.zeros(run_ref.shape, F32)

    src = lax.broadcasted_iota(I32, (nb, nb), 0)
    dst = lax.broadcasted_iota(I32, (nb, nb), 1)
    before = (src < dst).astype(BF16)
    run = run_ref[:, 0:1]
    for b in range(tr // nb):
        cols = slice(b * nb, (b + 1) * nb)
        logits_t = lg_ref[cols, :].T[0:ROUTER_ROWS]
        meta, gates, run = _route_block(logits_t, run, before)
        meta_ref[:, cols] = meta
        gate_ref[:, cols] = gates
    run_ref[...] = jnp.broadcast_to(run, run_ref.shape)
    cnt_ref[...] = run_ref[...]


def _route(logits, *, tr):
    t = logits.shape[0]
    return pl.pallas_call(
        _route_body,
        grid=(t // tr,),
        in_specs=[pl.BlockSpec((tr, LANES), lambda i: (i, 0))],
        out_specs=[pl.BlockSpec((SUBLANES, tr), lambda i: (0, i)),
                   pl.BlockSpec((SUBLANES, tr), lambda i: (0, i)),
                   pl.BlockSpec((N_EXPERTS, LANES), lambda i: (0, 0))],
        out_shape=[jax.ShapeDtypeStruct((SUBLANES, t), I32),
                   jax.ShapeDtypeStruct((SUBLANES, t), F32),
                   jax.ShapeDtypeStruct((N_EXPERTS, LANES), F32)],
        scratch_shapes=[pltpu.VMEM((N_EXPERTS, LANES), F32)],
        compiler_params=_params(1, 64 * tr * 4 * 4),
        name="route",
    )(logits)


def _pos_spec(tm):
    return pl.BlockSpec((1, 1, TOP_K * tm), lambda i: (i, 0, 0), memory_space=pltpu.SMEM)


def _wait_rows(src_ref, dst_ref, sem):
    pltpu.make_async_copy(src_ref, dst_ref, sem).wait()


def _scatter_body(pos_ref, h_ref, xs_ref, sem):
    tm = h_ref.shape[0]

    for r in range(tm):
        for k in range(TOP_K):
            p = pos_ref[0, 0, k * tm + r]
            pltpu.make_async_copy(h_ref.at[pl.ds(r, 1), :], xs_ref.at[pl.ds(p, 1), :],
                                  sem).start(priority=k)
    for k in range(TOP_K):
        _wait_rows(h_ref, xs_ref.at[pl.ds(0, tm), :], sem)


def _scatter_rows(hn, pos3, *, tm):
    t, d = hn.shape
    return pl.pallas_call(
        _scatter_body,
        grid=(t // tm,),
        in_specs=[_pos_spec(tm), pl.BlockSpec((tm, d), lambda i: (i, 0))],
        out_specs=pl.BlockSpec(memory_space=pl.ANY),
        out_shape=jax.ShapeDtypeStruct((TOP_K * t, d), hn.dtype),
        scratch_shapes=[pltpu.SemaphoreType.DMA],
        compiler_params=_params(1, 2 * tm * d * 4),
        name="scatter_rows",
    )(pos3, hn)


def _expert_body(vblk, vexp, vfirst, vnew, vvalid, vslot, vnext, vhalf, estart,
                 x_ref, w1_hbm, w3_hbm, w2_hbm, o_ref, wf1, wf3, wf2, w1b, w3b, w2b, sem, *, layer):
    bm = x_ref.shape[0]
    v = pl.program_id(0)
    weights = ((w1_hbm, wf1), (w3_hbm, wf3), (w2_hbm, wf2))

    def weight_copies(e, slot):
        return [pltpu.make_async_copy(w_hbm.at[layer, e], wf.at[slot], sem.at[slot, j])
                for j, (w_hbm, wf) in enumerate(weights)]

    @pl.when(v == 0)
    def _():
        for c in weight_copies(vexp[0], 0):
            c.start()

    @pl.when(vvalid[v] == 1)
    def _():
        @pl.when(vnew[v] == 1)
        def _():
            slot = vslot[v]
            for c in weight_copies(vexp[v], slot):
                c.wait()

            @pl.when(vnext[v] >= 0)
            def _():
                for c in weight_copies(vnext[v], 1 - slot):
                    c.start()

            w1b[...] = wf1[slot].astype(BF16)
            w3b[...] = wf3[slot].astype(BF16)
            w2b[...] = wf2[slot].astype(BF16)

        def visit(r0, n, r0_other, n_other):
            rows = slice(r0, r0 + n)
            xb = x_ref[rows, :].astype(BF16)
            h1 = jnp.dot(xb, w1b[...], preferred_element_type=F32)
            h3 = jnp.dot(xb, w3b[...], preferred_element_type=F32)
            hid = (h1 * _sigmoid(h1)) * h3
            y = jnp.dot(hid.astype(BF16), w2b[...], preferred_element_type=F32)

            @pl.when(vfirst[v] == 1)
            def _():
                o_ref[rows, :] = y
                if n_other:
                    o_ref[r0_other:r0_other + n_other, :] = jnp.zeros((n_other, o_ref.shape[1]), F32)

            @pl.when(vfirst[v] == 0)
            def _():
                e = vexp[v]
                r = vblk[v] * bm + r0 + lax.broadcasted_iota(I32, (n, 1), 0)
                mine = (r >= estart[e]) & (r < estart[e + 1])
                o_ref[rows, :] = jnp.where(mine, y, o_ref[rows, :])

        hb = bm // 2

        @pl.when(vhalf[v] == 0)
        def _():
            visit(0, bm, 0, 0)

        @pl.when(vhalf[v] == 1)
        def _():
            visit(0, hb, hb, bm - hb)

        @pl.when(vhalf[v] == 2)
        def _():
            visit(hb, bm - hb, 0, hb)


def _experts(xs, w1, w3, w2, tables, *, layer, bm):
    n, d = xs.shape
    f = w1.shape[3]
    n_visits = tables[0].shape[0]
    nbytes = 2 * (bm * d * 4 * 2 + 3 * d * f * 4) + 3 * d * f * 2 + 6 * bm * d * 4
    hbm = pl.BlockSpec(memory_space=pl.ANY)
    grid_spec = pltpu.PrefetchScalarGridSpec(
        num_scalar_prefetch=len(tables),
        grid=(n_visits,),
        in_specs=[pl.BlockSpec((bm, d), lambda v, blk, *_: (blk[v], 0)), hbm, hbm, hbm],
        out_specs=pl.BlockSpec((bm, d), lambda v, blk, *_: (blk[v], 0)),
        scratch_shapes=[pltpu.VMEM((2, d, f), F32), pltpu.VMEM((2, d, f), F32), pltpu.VMEM((2, f, d), F32),
                        pltpu.VMEM((d, f), BF16), pltpu.VMEM((d, f), BF16), pltpu.VMEM((f, d), BF16),
                        pltpu.SemaphoreType.DMA((2, 3))])
    return pl.pallas_call(
        functools.partial(_expert_body, layer=layer),
        grid_spec=grid_spec,
        out_shape=jax.ShapeDtypeStruct((n, d), F32),
        compiler_params=_params(1, nbytes),
        name="experts",
    )(*tables, xs, w1, w3, w2)


def _visit_tables(counts, n_slots, bm):
    n_blocks = n_slots // bm
    n_visits = n_blocks + N_EXPERTS - 1
    end = jnp.cumsum(counts)
    start = end - counts
    first_blk = start // bm
    last_blk = jnp.maximum(end - 1, 0) // bm
    per_e = jnp.where(counts > 0, last_blk - first_blk + 1, 0)
    cum = jnp.cumsum(per_e)
    total = cum[-1]
    v = jnp.minimum(jnp.arange(n_visits, dtype=I32), total - 1)
    e = jnp.sum((cum[None, :] <= v[:, None]).astype(I32), axis=1)
    blk = (first_blk[e] + (v - (cum[e] - per_e[e]))).astype(I32)
    valid = (jnp.arange(n_visits) < total).astype(I32)
    prev_blk = jnp.concatenate([jnp.full((1,), -1, I32), blk[:-1]])
    prev_e = jnp.concatenate([jnp.full((1,), -1, I32), e[:-1]])
    first = (blk != prev_blk).astype(I32)
    new = (e != prev_e).astype(I32)
    ordinal = jnp.cumsum(new) - 1
    slot = (ordinal % 2).astype(I32)
    is_next = (new[None, :] == 1) & (ordinal[None, :] == ordinal[:, None] + 1)
    nxt = jnp.where(jnp.any(is_next, axis=1), jnp.sum(jnp.where(is_next, e[None, :], 0), axis=1), -1)
    lo = jnp.maximum(start[e], blk * bm) - blk * bm
    hi = jnp.minimum(end[e], (blk + 1) * bm) - blk * bm
    half = jnp.where(hi <= bm // 2, 1, jnp.where(lo >= bm // 2, 2, 0)).astype(I32)
    estart = jnp.concatenate([start, end[-1:]]).astype(I32)
    return blk, e, first, new, valid, slot, nxt.astype(I32), half, estart


def _combine_body(pos_ref, x_ref, gt_ref, g_ref, ys_ref, o_ref, ybuf, sem):
    tm = x_ref.shape[0]
    for r in range(tm):
        for k in range(TOP_K):
            p = pos_ref[0, 0, k * tm + r]
            pltpu.make_async_copy(ys_ref.at[pl.ds(p, 1), :], ybuf.at[k, pl.ds(r, 1), :],
                                  sem).start(priority=k)
    for k in range(TOP_K):
        _wait_rows(ys_ref.at[pl.ds(0, tm), :], ybuf.at[k], sem)
    x2 = x_ref[...] + gt_ref[:, 0:1] * ybuf[0] + gt_ref[:, 1:2] * ybuf[1]
    o_ref[...] = _rms_norm(x2, g_ref[...])


def _combine(x1, ys, pos3, gates_t, g_final, *, tm):
    t, d = x1.shape
    nbytes = 2 * (2 * tm * d * 4 + tm * LANES * 4) + TOP_K * tm * d * 4 + 2 * tm * d * 4
    return pl.pallas_call(
        _combine_body,
        grid=(t // tm,),
        in_specs=[_pos_spec(tm),
                  pl.BlockSpec((tm, d), lambda i: (i, 0)),
                  pl.BlockSpec((tm, TOP_K), lambda i: (i, 0)),
                  pl.BlockSpec((1, d), lambda i: (0, 0)),
                  pl.BlockSpec(memory_space=pl.ANY)],
        out_specs=pl.BlockSpec((tm, d), lambda i: (i, 0)),
        out_shape=jax.ShapeDtypeStruct((t, d), F32),
        scratch_shapes=[pltpu.VMEM((TOP_K, tm, d), F32), pltpu.SemaphoreType.DMA],
        compiler_params=_params(1, nbytes),
        name="combine",
    )(pos3, x1, gates_t, g_final.reshape(1, d), ys)


def _tiles(t):
    return dict(tm_front=min(256, t), tr=min(2048, t), tm_scatter=min(1024, t), tm_combine=min(1024, t),
                bm=min(256, TOP_K * t))


def _tile_pos(pos, tm):
    t = pos.shape[1]
    return pos.reshape(TOP_K, t // tm, tm).transpose(1, 0, 2).reshape(t // tm, 1, TOP_K * tm)


def _router_params(w_group, b_group, w_router, b_router):
    d = w_group.shape[0]
    gpad = SUBLANES - N_GROUPS
    wr = jnp.concatenate([w_group, jnp.zeros((d, gpad), F32), w_router,
                          jnp.zeros((d, LANES - ROUTER_ROWS), F32)], axis=1)
    br = jnp.concatenate([b_group, jnp.zeros((gpad,), F32), b_router,
                          jnp.zeros((LANES - ROUTER_ROWS,), F32)]).reshape(1, LANES)
    return wr, br


def _moe(x1, hn, logits, w1, w3, w2, layer, tl):
    t, d = x1.shape
    meta, gates, cnt = _route(logits, tr=tl["tr"])

    counts = cnt[:, 0].astype(I32)
    start = jnp.cumsum(counts) - counts
    is_e = meta[0:TOP_K, :, None] == jnp.arange(N_EXPERTS, dtype=I32)
    pos = jnp.sum(jnp.where(is_e, start, 0), axis=-1) + meta[TOP_K:2 * TOP_K]
    xs = _scatter_rows(hn, _tile_pos(pos, tl["tm_scatter"]), tm=tl["tm_scatter"])
    tables = _visit_tables(counts, TOP_K * t, tl["bm"])
    ys = _experts(xs, w1, w3, w2, tables, layer=layer, bm=tl["bm"])
    return x1, gates[0:TOP_K].T, pos, ys


def kernel(x, ln_mix, ln_ffn, ln_final, ev_w_in, ev_conv_w, ev_sgu_ln_g, ev_sgu_ln_b, ev_sgu_w, ev_sgu_b, ev_w_out, od_w_in, od_pool_w, od_pool_scale, od_dw_w, od_dw_b, od_ln_g, od_ln_b, od_w_out, moe_w_group, moe_b_group, moe_w_router, moe_b_router, moe_w1, moe_w3, moe_w2):
    bsz, s, d = x.shape
    assert bsz == 1, "token mixers carry state across row tiles of one sequence"
    depth = ln_mix.shape[0]
    tl = _tiles(s)
    tm = tl["tm_front"]
    x_in = x.reshape(s, d)
    for layer in range(depth):
        i = layer // 2
        if layer > 0:
            x1, gates_t, pos, ys = x_in
            x_in = (x1, gates_t, _tile_pos(pos, tm), ys)
        wr, br = _router_params(moe_w_group[layer], moe_b_group[layer],
                                moe_w_router[layer], moe_b_router[layer])
        if layer % 2 == 0:
            c = ev_conv_w.shape[2]
            mix_params = [ev_conv_w[i], ev_sgu_ln_g[i].reshape(1, c), ev_sgu_ln_b[i].reshape(1, c),
                          ev_sgu_w[i], ev_sgu_b[i].T]
            mix_scratch = [pltpu.VMEM((tm + EVEN_HALO, c), F32)]
            x1, hn, logits = _layer_front(
                "even_layer", _even_mix, x_in, ln_mix[layer], ev_w_in[i].astype(BF16), mix_params,
                mix_scratch, ev_w_out[i].astype(BF16), ln_ffn[layer], wr, br, tm=tm)
        else:
            c = od_dw_w.shape[2]
            mix_params = [od_pool_w[i], od_pool_scale[i].reshape(1, c), od_dw_w[i],
                          od_dw_b[i].reshape(1, c), od_ln_g[i].reshape(1, c), od_ln_b[i].reshape(1, c)]
            gd = c // len(POOL_WINDOWS)
            mix_scratch = [pltpu.VMEM((tm + POOL_HALO, c), F32),
                           pltpu.VMEM((tm + POOL_HALO, gd), F32), pltpu.VMEM((tm + POOL_HALO, gd), F32),
                           pltpu.VMEM((tm + CONF_HALO, c), F32),
                           pltpu.VMEM((SUBLANES - 1, tm + CONF_HALO, LANES), F32),
                           pltpu.VMEM((tm, c), F32)]
            x1, hn, logits = _layer_front(
                "odd_layer", _odd_mix, x_in, ln_mix[layer], od_w_in[i].astype(BF16), mix_params,
                mix_scratch, od_w_out[i].astype(BF16), ln_ffn[layer], wr, br, tm=tm)
        x_in = _moe(x1, hn, logits, moe_w1, moe_w3, moe_w2, layer, tl)
    x1, gates_t, pos, ys = x_in
    out = _combine(x1, ys, _tile_pos(pos, tl["tm_combine"]), gates_t, ln_final, tm=tl["tm_combine"])
    return out.reshape(bsz, s, d)
```

```python
import functools
import math

import jax
import jax.numpy as jnp
from jax import lax
from jax.experimental import pallas as pl
from jax.experimental.pallas import tpu as pltpu

F32 = jnp.float32
BF16 = jnp.bfloat16
I32 = jnp.int32

EPS = 1e-6
CHUNK = 64
SGU_BLOCK = 128
SGU_HEADS = 8
POOL_WINDOWS = (2, 4, 8, 16)
N_GROUPS = 4
EXPERTS_PER_GROUP = 8
N_EXPERTS = N_GROUPS * EXPERTS_PER_GROUP
TOP_K = 2

LANES = 128
SUBLANES = 8
VMEM_CAP_BYTES = 60 * 1024 * 1024

ROUTER_ROWS = SUBLANES + N_EXPERTS
NEG_BIG = -1e30


def _vmem_limit(nbytes):
    return int(min(VMEM_CAP_BYTES, nbytes * 5 // 4 + (6 << 20)))


def _params(n_axes, nbytes):
    return pltpu.CompilerParams(
        dimension_semantics=("arbitrary",) * n_axes,
        vmem_limit_bytes=_vmem_limit(nbytes))


def _const_spec(shape, single_buffer=False):
    mode = dict(pipeline_mode=pl.Buffered(1)) if single_buffer else {}
    return pl.BlockSpec(shape, lambda i: (0,) * len(shape), **mode)


def _sigmoid(x):
    return 1.0 / (1.0 + jnp.exp(-x))


def _gelu_tanh(x):
    c = math.sqrt(2.0 / math.pi)
    return x * (0.5 * (1.0 + jnp.tanh(c * (x + 0.044715 * (x * x * x)))))


def _layer_norm(v, g, b):
    mu = jnp.mean(v, axis=-1, keepdims=True)
    vc = v - mu
    var = jnp.mean(vc * vc, axis=-1, keepdims=True)
    return vc * lax.rsqrt(var + EPS) * g + b


def _rms_norm(x, g):
    ms = jnp.mean(x * x, axis=-1, keepdims=True)
    return x * lax.rsqrt(ms + EPS) * g


def _split_bf16(v):
    head = v.astype(BF16)
    return head, (v - head.astype(F32)).astype(BF16)


IN_PROJ_COLS = 1024


def _in_proj(x_ref, g_ref, w_ref, z_ref):
    h = _rms_norm(x_ref[...], g_ref[...]).astype(BF16)
    for c0 in range(0, w_ref.shape[1], IN_PROJ_COLS):
        cols = slice(c0, c0 + IN_PROJ_COLS)
        z_ref[:, cols] = jnp.dot(h, w_ref[:, cols], preferred_element_type=F32)


def _out_proj_route(x_ref, y_ref, w_ref, g_ref, wh_ref, wl_ref, br_ref, x1_ref, lg_ref):
    x1 = x_ref[...] + jnp.dot(y_ref[...], w_ref[...], preferred_element_type=F32)
    x1_ref[...] = x1
    hn = _rms_norm(x1, g_ref[...])
    head, rem = _split_bf16(hn)
    lg_ref[...] = (jnp.dot(head, wh_ref[...], preferred_element_type=F32)
                   + jnp.dot(rem, wh_ref[...], preferred_element_type=F32)
                   + jnp.dot(head, wl_ref[...], preferred_element_type=F32)
                   + br_ref[...])


EVEN_HALO = SUBLANES


def _even_mix(z_ref, cw_ref, lg_ref, lb_ref, sw_ref, sb_ref, o_ref, pbuf):
    tm = z_ref.shape[0]
    kw, c = cw_ref.shape
    col = lambda j: z_ref[:, j * c:(j + 1) * c]

    @pl.when(pl.program_id(0) == 0)
    def _():
        pbuf[0:EVEN_HALO, :] = jnp.zeros((EVEN_HALO, c), F32)

    p = col(1) * col(2)
    pbuf[EVEN_HALO:EVEN_HALO + tm, :] = p
    conv = cw_ref[kw - 1:kw, :] * p
    for k in range(kw - 1):
        conv = conv + cw_ref[k:k + 1, :] * pbuf[pl.ds(EVEN_HALO - (kw - 1) + k, tm), :]
    o_ref[:, 0:c] = (col(0) * conv).astype(BF16)
    pbuf[0:EVEN_HALO, :] = pbuf[tm:tm + EVEN_HALO, :]

    u = _gelu_tanh(col(3))
    vn = _layer_norm(_gelu_tanh(col(4)), lg_ref[...], lb_ref[...]).astype(BF16)
    tpos = lax.broadcasted_iota(I32, (SGU_BLOCK, SGU_BLOCK), 0)
    spos = lax.broadcasted_iota(I32, (SGU_BLOCK, SGU_BLOCK), 1)
    mask = (spos // CHUNK) <= (tpos // CHUNK)
    hd = c // SGU_HEADS
    for h in range(SGU_HEADS):
        wh = jnp.where(mask, sw_ref[h], 0.0).astype(BF16)
        bh = sb_ref[:, h:h + 1]
        cols = slice(h * hd, (h + 1) * hd)
        for n in range(tm // SGU_BLOCK):
            rows = slice(n * SGU_BLOCK, (n + 1) * SGU_BLOCK)
            sv = jnp.dot(wh, vn[rows, cols], preferred_element_type=F32) + bh
            o_ref[rows, c + h * hd:c + (h + 1) * hd] = (u[rows, cols] * sv).astype(BF16)


POOL_HALO = 24
CONF_HALO = 32
CONV_ROWS = 128


def _window_sum(xbuf, cols, w, lv_a, lv_b, tm):
    lo, hi = SUBLANES, POOL_HALO + tm
    src, src_cols, n = xbuf, cols, 1
    dst, other = lv_a, lv_b
    while 2 * n < w:
        dst[lo:hi, :] = src[lo:hi, src_cols] + src[pl.ds(lo - n, hi - lo), src_cols]
        src, src_cols, n = dst, slice(None), 2 * n
        dst, other = other, dst
    return src[POOL_HALO:hi, src_cols] + src[pl.ds(POOL_HALO - n, tm), src_cols]


def _odd_mix(z_ref, pw_ref, ps_ref, dw_ref, db_ref, lg_ref, lb_ref, o_ref,
             xbuf, lv_a, lv_b, gbuf, sbuf, dbuf):
    tm = z_ref.shape[0]
    kw, c = dw_ref.shape
    i = pl.program_id(0)

    @pl.when(i == 0)
    def _():
        xbuf[0:POOL_HALO, :] = jnp.zeros((POOL_HALO, c), F32)
        gbuf[0:CONF_HALO, :] = jnp.zeros((CONF_HALO, c), F32)
        lv_a[0:SUBLANES, :] = jnp.zeros((SUBLANES, lv_a.shape[1]), F32)
        lv_b[0:SUBLANES, :] = jnp.zeros((SUBLANES, lv_b.shape[1]), F32)

    x = z_ref[:, 0:c]
    xbuf[POOL_HALO:POOL_HALO + tm, :] = x
    step = i * tm + lax.broadcasted_iota(I32, (tm, 1), 0) + 1
    gd = c // len(POOL_WINDOWS)
    for g, w in enumerate(POOL_WINDOWS):
        cols = slice(g * gd, (g + 1) * gd)
        s = _window_sum(xbuf, cols, w, lv_a, lv_b, tm)
        inv_count = 1.0 / jnp.minimum(step, w).astype(F32)
        d = s * inv_count - x[:, cols]
        y = jnp.dot(d.astype(BF16), pw_ref[g].astype(BF16), preferred_element_type=F32)
        o_ref[:, cols] = (y * ps_ref[:, cols]).astype(BF16)
    xbuf[0:POOL_HALO, :] = xbuf[tm:tm + POOL_HALO, :]

    gbuf[CONF_HALO:CONF_HALO + tm, :] = z_ref[:, c:2 * c] * _sigmoid(z_ref[:, 2 * c:3 * c])
    n_rows = CONF_HALO + tm

    def conv_cols(cb, carry):
        lanes = pl.ds(pl.multiple_of(cb * LANES, LANES), LANES)
        for b in range(1, SUBLANES):
            sbuf[b - 1, SUBLANES:n_rows, :] = gbuf[pl.ds(SUBLANES - b, n_rows - SUBLANES), lanes]
        accs = [jnp.broadcast_to(db_ref[:, lanes], (CONV_ROWS, LANES)) for _ in range(0, tm, CONV_ROWS)]
        for back in range(kw):
            a, b = divmod(back, SUBLANES)
            wk = dw_ref[kw - 1 - back:kw - back, lanes]
            for n, r0 in enumerate(range(0, tm, CONV_ROWS)):
                start = r0 + CONF_HALO - SUBLANES * a
                tap = gbuf[start:start + CONV_ROWS, lanes] if b == 0 else sbuf[b - 1, start:start + CONV_ROWS, :]
                accs[n] = accs[n] + wk * tap
        for n, r0 in enumerate(range(0, tm, CONV_ROWS)):
            dbuf[r0:r0 + CONV_ROWS, lanes] = accs[n]
        return carry

    lax.fori_loop(0, c // LANES, conv_cols, 0)
    gbuf[0:CONF_HALO, :] = gbuf[tm:tm + CONF_HALO, :]
    yn = _layer_norm(dbuf[...], lg_ref[...], lb_ref[...])
    o_ref[:, c:2 * c] = (yn * _sigmoid(yn)).astype(BF16)


def _start_row_gathers(pos_ref, ys_ref, ybuf, sem, slot):
    tm = ybuf.shape[2]
    for r in range(tm):
        for k in range(TOP_K):
            p = pos_ref[0, 0, k * tm + r]
            pltpu.make_async_copy(ys_ref.at[pl.ds(p, 1), :], ybuf.at[slot, k, pl.ds(r, 1), :],
                                  sem.at[slot]).start(priority=k)


def _wait_row_gathers(ys_ref, ybuf, sem, slot):
    tm = ybuf.shape[2]
    for k in range(TOP_K):
        _wait_rows(ys_ref.at[pl.ds(0, tm), :], ybuf.at[slot, k], sem.at[slot])


def _gathered_input(pos_cur, pos_next, x1_ref, gt_ref, ys_ref, x_ref, ybuf, sem):
    i = pl.program_id(0)
    slot = i % 2

    @pl.when(i == 0)
    def _():
        _start_row_gathers(pos_cur, ys_ref, ybuf, sem, 0)

    _wait_row_gathers(ys_ref, ybuf, sem, slot)
    x_ref[...] = x1_ref[...] + gt_ref[:, 0:1] * ybuf[slot, 0] + gt_ref[:, 1:2] * ybuf[slot, 1]
    _start_row_gathers(pos_next, ys_ref, ybuf, sem, 1 - slot)


def _front_body(*refs, mix, n_mix_params, gathered):
    n_head = 5 if gathered else 1
    head, refs = refs[:n_head], refs[n_head:]
    gm_ref, win_ref = refs[0:2]
    mix_params = refs[2:2 + n_mix_params]
    wout_ref, gf_ref, wh_ref, wl_ref, br_ref, x1_ref, lgt_ref, z_ref, y_ref = (
        refs[2 + n_mix_params:11 + n_mix_params])
    scratch = refs[11 + n_mix_params:]
    if gathered:
        mix_scratch, (x_ref, ybuf, sem) = scratch[:-3], scratch[-3:]
        _gathered_input(*head, x_ref, ybuf, sem)
    else:
        mix_scratch, x_ref = scratch, head[0]
    _in_proj(x_ref, gm_ref, win_ref, z_ref)
    mix(z_ref, *mix_params, y_ref, *mix_scratch)
    _out_proj_route(x_ref, y_ref, wout_ref, gf_ref, wh_ref, wl_ref, br_ref, x1_ref, lgt_ref)
    if gathered:
        @pl.when(pl.program_id(0) == pl.num_programs(0) - 1)
        def _():
            _wait_row_gathers(head[4], ybuf, sem, 1 - pl.program_id(0) % 2)


def _layer_front(name, mix, x_in, g_mix, w_in, mix_params, mix_scratch, w_out, g_ffn, wr, br, *, tm):
    gathered = isinstance(x_in, tuple)
    t, d = (x_in[0] if gathered else x_in).shape
    nt = t // tm
    n = w_in.shape[1]
    c2 = w_out.shape[0]
    wh, wl = _split_bf16(wr)
    row = lambda width: pl.BlockSpec((tm, width), lambda i: (i, 0))
    tail = [g_ffn.reshape(1, d), wh, wl, br]
    scratch = [pltpu.VMEM((tm, n), F32), pltpu.VMEM((tm, c2), BF16)] + mix_scratch
    if gathered:
        x1, gates_t, pos3, ys = x_in
        head = [pos3, pos3, x1, gates_t, ys]
        head_specs = [_pos_spec(tm),
                      pl.BlockSpec((1, 1, TOP_K * tm), lambda i: (jnp.minimum(i + 1, nt - 1), 0, 0),
                                   memory_space=pltpu.SMEM),
                      row(d), pl.BlockSpec((tm, TOP_K), lambda i: (i, 0)),
                      pl.BlockSpec(memory_space=pl.ANY)]
        scratch += [pltpu.VMEM((tm, d), F32), pltpu.VMEM((2, TOP_K, tm, d), F32),
                    pltpu.SemaphoreType.DMA((2,))]
    else:
        head, head_specs = [x_in], [row(d)]
    nbytes = (d * n * 2 + c2 * d * 2
              + 2 * 2 * tm * d * 4 + 2 * tm * LANES * 4
              + tm * n * 4 + tm * c2 * 2
              + sum(math.prod(s.shape) * 4 for s in mix_scratch)
              + 2 * sum(p.size * p.dtype.itemsize for p in mix_params + tail[1:])
              + (2 * TOP_K + 1) * tm * d * 4 * gathered
              + 6 * tm * d * 4)
    return pl.pallas_call(
        functools.partial(_front_body, mix=mix, n_mix_params=len(mix_params), gathered=gathered),
        grid=(nt,),
        in_specs=(head_specs + [_const_spec((1, d)), _const_spec(w_in.shape, single_buffer=True)]
                  + [_const_spec(p.shape) for p in mix_params]
                  + [_const_spec(w_out.shape, single_buffer=True), _const_spec((1, d)),
                     _const_spec(wh.shape), _const_spec(wl.shape), _const_spec(br.shape)]),
        out_specs=[row(d), row(LANES)],
        out_shape=[jax.ShapeDtypeStruct((t, d), F32),
                   jax.ShapeDtypeStruct((t, LANES), F32)],
        scratch_shapes=scratch,
        compiler_params=_params(1, nbytes),
        name=name,
    )(*head, g_mix.reshape(1, d), w_in, *mix_params, w_out, *tail)


RANK_BLOCK = 256


def _route_block(logits, run, before):
    n = logits.shape[1]
    row8 = lax.broadcasted_iota(I32, (SUBLANES, n), 0)
    row = lax.broadcasted_iota(I32, (N_EXPERTS, n), 0)

    gl = jnp.where(row8 < N_GROUPS, logits[0:SUBLANES], NEG_BIG)
    gmax = jnp.max(gl, axis=0, keepdims=True)
    gidx = jnp.min(jnp.where(gl == gmax, row8, N_EXPERTS), axis=0, keepdims=True)
    p_group = 1.0 / jnp.sum(jnp.exp(gl - gmax), axis=0, keepdims=True)

    el = jnp.where(row // EXPERTS_PER_GROUP == gidx, logits[SUBLANES:ROUTER_ROWS], NEG_BIG)
    m0 = jnp.max(el, axis=0, keepdims=True)
    e0 = jnp.min(jnp.where(el == m0, row, N_EXPERTS), axis=0, keepdims=True)
    el1 = jnp.where(row == e0, NEG_BIG, el)
    m1 = jnp.max(el1, axis=0, keepdims=True)
    e1 = jnp.min(jnp.where(el1 == m1, row, N_EXPERTS), axis=0, keepdims=True)
    ratio = jnp.exp(m1 - m0)
    gate0 = p_group / (1.0 + ratio)
    gate1 = p_group * ratio / (1.0 + ratio)

    sel0 = row == e0
    sel1 = row == e1
    pre0 = jnp.dot(sel0.astype(BF16), before, preferred_element_type=F32)
    pre1 = jnp.dot(sel1.astype(BF16), before, preferred_element_type=F32)
    cnt0 = jnp.sum(sel0.astype(F32), axis=1, keepdims=True)
    cnt1 = jnp.sum(sel1.astype(F32), axis=1, keepdims=True)
    rank0 = jnp.sum(jnp.where(sel0, pre0 + run, 0.0), axis=0, keepdims=True)
    rank1 = jnp.sum(jnp.where(sel1, pre1 + (run + cnt0), 0.0), axis=0, keepdims=True)

    meta = jnp.where(row8 == 0, e0, jnp.where(row8 == 1, e1, jnp.where(
        row8 == 2, rank0.astype(I32), jnp.where(row8 == 3, rank1.astype(I32), 0))))
    gates = jnp.where(row8 == 0, gate0, jnp.where(row8 == 1, gate1, 0.0))
    return meta, gates, run + (cnt0 + cnt1)


def _route_body(lg_ref, meta_ref, gate_ref, cnt_ref, run_ref):
    tr = lg_ref.shape[0]
    nb = min(RANK_BLOCK, tr)

    @pl.when(pl.program_id(0) == 0)
    def _():
        run_ref[...] = jnp.zeros(run_ref.shape, F32)

    src = lax.broadcasted_iota(I32, (nb, nb), 0)
    dst = lax.broadcasted_iota(I32, (nb, nb), 1)
    before = (src < dst).astype(BF16)
    run = run_ref[:, 0:1]
    for b in range(tr // nb):
        cols = slice(b * nb, (b + 1) * nb)
        logits_t = lg_ref[cols, :].T[0:ROUTER_ROWS]
        meta, gates, run = _route_block(logits_t, run, before)
        meta_ref[:, cols] = meta
        gate_ref[:, cols] = gates
    run_ref[...] = jnp.broadcast_to(run, run_ref.shape)
    cnt_ref[...] = run_ref[...]


def _route(logits, *, tr):
    t = logits.shape[0]
    return pl.pallas_call(
        _route_body,
        grid=(t // tr,),
        in_specs=[pl.BlockSpec((tr, LANES), lambda i: (i, 0))],
        out_specs=[pl.BlockSpec((SUBLANES, tr), lambda i: (0, i)),
                   pl.BlockSpec((SUBLANES, tr), lambda i: (0, i)),
                   pl.BlockSpec((N_EXPERTS, LANES), lambda i: (0, 0))],
        out_shape=[jax.ShapeDtypeStruct((SUBLANES, t), I32),
                   jax.ShapeDtypeStruct((SUBLANES, t), F32),
                   jax.ShapeDtypeStruct((N_EXPERTS, LANES), F32)],
        scratch_shapes=[pltpu.VMEM((N_EXPERTS, LANES), F32)],
        compiler_params=_params(1, 64 * tr * 4 * 4),
        name="route",
    )(logits)


def _pos_spec(tm):
    return pl.BlockSpec((1, 1, TOP_K * tm), lambda i: (i, 0, 0), memory_space=pltpu.SMEM)


def _wait_rows(src_ref, dst_ref, sem):
    pltpu.make_async_copy(src_ref, dst_ref, sem).wait()


def _scatter_body(pos_ref, x_ref, g_ref, xs_ref, h_ref, sem):
    tm = x_ref.shape[0]
    h_ref[...] = _rms_norm(x_ref[...], g_ref[...])

    for r in range(tm):
        for k in range(TOP_K):
            p = pos_ref[0, 0, k * tm + r]
            pltpu.make_async_copy(h_ref.at[pl.ds(r, 1), :], xs_ref.at[pl.ds(p, 1), :],
                                  sem).start(priority=k)
    for k in range(TOP_K):
        _wait_rows(h_ref, xs_ref.at[pl.ds(0, tm), :], sem)


def _scatter_rows(x1, g, pos3, *, tm):
    t, d = x1.shape
    return pl.pallas_call(
        _scatter_body,
        grid=(t // tm,),
        in_specs=[_pos_spec(tm), pl.BlockSpec((tm, d), lambda i: (i, 0)), _const_spec((1, d))],
        out_specs=pl.BlockSpec(memory_space=pl.ANY),
        out_shape=jax.ShapeDtypeStruct((TOP_K * t, d), F32),
        scratch_shapes=[pltpu.VMEM((tm, d), F32), pltpu.SemaphoreType.DMA],
        compiler_params=_params(1, 4 * tm * d * 4),
        name="scatter_rows",
    )(pos3, x1, g.reshape(1, d))


def _expert_body(vblk, vexp, vfirst, vnew, vvalid, vslot, vnext, vhalf, estart,
                 x_ref, w1_hbm, w3_hbm, w2_hbm, o_ref, wf1, wf3, wf2, w1b, w3b, w2b, sem, *, layer):
    bm = x_ref.shape[0]
    v = pl.program_id(0)
    weights = ((w1_hbm, wf1), (w3_hbm, wf3), (w2_hbm, wf2))

    def weight_copies(e, slot):
        return [pltpu.make_async_copy(w_hbm.at[layer, e], wf.at[slot], sem.at[slot, j])
                for j, (w_hbm, wf) in enumerate(weights)]

    @pl.when(v == 0)
    def _():
        for c in weight_copies(vexp[0], 0):
            c.start()

    @pl.when(vvalid[v] == 1)
    def _():
        @pl.when(vnew[v] == 1)
        def _():
            slot = vslot[v]
            for c in weight_copies(vexp[v], slot):
                c.wait()

            @pl.when(vnext[v] >= 0)
            def _():
                for c in weight_copies(vnext[v], 1 - slot):
                    c.start()

            w1b[...] = wf1[slot].astype(BF16)
            w3b[...] = wf3[slot].astype(BF16)
            w2b[...] = wf2[slot].astype(BF16)

        def visit(r0, n, r0_other, n_other):
            rows = slice(r0, r0 + n)
            xb = x_ref[rows, :].astype(BF16)
            h1 = jnp.dot(xb, w1b[...], preferred_element_type=F32)
            h3 = jnp.dot(xb, w3b[...], preferred_element_type=F32)
            hid = (h1 * _sigmoid(h1)) * h3
            y = jnp.dot(hid.astype(BF16), w2b[...], preferred_element_type=F32)

            @pl.when(vfirst[v] == 1)
            def _():
                o_ref[rows, :] = y
                if n_other:
                    o_ref[r0_other:r0_other + n_other, :] = jnp.zeros((n_other, o_ref.shape[1]), F32)

            @pl.when(vfirst[v] == 0)
            def _():
                e = vexp[v]
                r = vblk[v] * bm + r0 + lax.broadcasted_iota(I32, (n, 1), 0)
                mine = (r >= estart[e]) & (r < estart[e + 1])
                o_ref[rows, :] = jnp.where(mine, y, o_ref[rows, :])

        hb = bm // 2

        @pl.when(vhalf[v] == 0)
        def _():
            visit(0, bm, 0, 0)

        @pl.when(vhalf[v] == 1)
        def _():
            visit(0, hb, hb, bm - hb)

        @pl.when(vhalf[v] == 2)
        def _():
            visit(hb, bm - hb, 0, hb)


def _experts(xs, w1, w3, w2, tables, *, layer, bm):
    n, d = xs.shape
    f = w1.shape[3]
    n_visits = tables[0].shape[0]
    nbytes = 2 * (bm * d * 4 * 2 + 3 * d * f * 4) + 3 * d * f * 2 + 6 * bm * d * 4
    hbm = pl.BlockSpec(memory_space=pl.ANY)
    grid_spec = pltpu.PrefetchScalarGridSpec(
        num_scalar_prefetch=len(tables),
        grid=(n_visits,),
        in_specs=[pl.BlockSpec((bm, d), lambda v, blk, *_: (blk[v], 0)), hbm, hbm, hbm],
        out_specs=pl.BlockSpec((bm, d), lambda v, blk, *_: (blk[v], 0)),
        scratch_shapes=[pltpu.VMEM((2, d, f), F32), pltpu.VMEM((2, d, f), F32), pltpu.VMEM((2, f, d), F32),
                        pltpu.VMEM((d, f), BF16), pltpu.VMEM((d, f), BF16), pltpu.VMEM((f, d), BF16),
                        pltpu.SemaphoreType.DMA((2, 3))])
    return pl.pallas_call(
        functools.partial(_expert_body, layer=layer),
        grid_spec=grid_spec,
        out_shape=jax.ShapeDtypeStruct((n, d), F32),
        compiler_params=_params(1, nbytes),
        name="experts",
    )(*tables, xs, w1, w3, w2)


def _visit_tables(counts, n_slots, bm):
    n_blocks = n_slots // bm
    n_visits = n_blocks + N_EXPERTS - 1
    end = jnp.cumsum(counts)
    start = end - counts
    first_blk = start // bm
    last_blk = jnp.maximum(end - 1, 0) // bm
    per_e = jnp.where(counts > 0, last_blk - first_blk + 1, 0)
    cum = jnp.cumsum(per_e)
    total = cum[-1]
    v = jnp.minimum(jnp.arange(n_visits, dtype=I32), total - 1)
    e = jnp.sum((cum[None, :] <= v[:, None]).astype(I32), axis=1)
    blk = (first_blk[e] + (v - (cum[e] - per_e[e]))).astype(I32)
    valid = (jnp.arange(n_visits) < total).astype(I32)
    prev_blk = jnp.concatenate([jnp.full((1,), -1, I32), blk[:-1]])
    prev_e = jnp.concatenate([jnp.full((1,), -1, I32), e[:-1]])
    first = (blk != prev_blk).astype(I32)
    new = (e != prev_e).astype(I32)
    ordinal = jnp.cumsum(new) - 1
    slot = (ordinal % 2).astype(I32)
    is_next = (new[None, :] == 1) & (ordinal[None, :] == ordinal[:, None] + 1)
    nxt = jnp.where(jnp.any(is_next, axis=1), jnp.sum(jnp.where(is_next, e[None, :], 0), axis=1), -1)
    lo = jnp.maximum(start[e], blk * bm) - blk * bm
    hi = jnp.minimum(end[e], (blk + 1) * bm) - blk * bm
    half = jnp.where(hi <= bm // 2, 1, jnp.where(lo >= bm // 2, 2, 0)).astype(I32)
    estart = jnp.concatenate([start, end[-1:]]).astype(I32)
    return blk, e, first, new, valid, slot, nxt.astype(I32), half, estart


def _combine_body(pos_cur, pos_next, x_ref, gt_ref, g_ref, ys_ref, o_ref, ybuf, sem):
    i = pl.program_id(0)
    slot = i % 2

    @pl.when(i == 0)
    def _():
        _start_row_gathers(pos_cur, ys_ref, ybuf, sem, 0)

    _wait_row_gathers(ys_ref, ybuf, sem, slot)
    _start_row_gathers(pos_next, ys_ref, ybuf, sem, 1 - slot)
    x2 = x_ref[...] + gt_ref[:, 0:1] * ybuf[slot, 0] + gt_ref[:, 1:2] * ybuf[slot, 1]
    o_ref[...] = _rms_norm(x2, g_ref[...])

    @pl.when(i == pl.num_programs(0) - 1)
    def _():
        _wait_row_gathers(ys_ref, ybuf, sem, 1 - slot)


def _combine(x1, ys, pos3, gates_t, g_final, *, tm):
    t, d = x1.shape
    nt = t // tm
    nbytes = 2 * (2 * tm * d * 4 + tm * LANES * 4) + 2 * TOP_K * tm * d * 4 + 2 * tm * d * 4
    return pl.pallas_call(
        _combine_body,
        grid=(nt,),
        in_specs=[_pos_spec(tm),
                  pl.BlockSpec((1, 1, TOP_K * tm), lambda i: (jnp.minimum(i + 1, nt - 1), 0, 0),
                               memory_space=pltpu.SMEM),
                  pl.BlockSpec((tm, d), lambda i: (i, 0)),
                  pl.BlockSpec((tm, TOP_K), lambda i: (i, 0)),
                  pl.BlockSpec((1, d), lambda i: (0, 0)),
                  pl.BlockSpec(memory_space=pl.ANY)],
        out_specs=pl.BlockSpec((tm, d), lambda i: (i, 0)),
        out_shape=jax.ShapeDtypeStruct((t, d), F32),
        scratch_shapes=[pltpu.VMEM((2, TOP_K, tm, d), F32), pltpu.SemaphoreType.DMA((2,))],
        compiler_params=_params(1, nbytes),
        name="combine",
    )(pos3, pos3, x1, gates_t, g_final.reshape(1, d), ys)


def _tiles(t):
    return dict(tm_front=min(256, t), tr=min(2048, t), tm_scatter=min(1024, t), tm_combine=min(512, t),
                bm=min(256, TOP_K * t))


def _tile_pos(pos, tm):
    t = pos.shape[1]
    return pos.reshape(TOP_K, t // tm, tm).transpose(1, 0, 2).reshape(t // tm, 1, TOP_K * tm)


def _router_params(w_group, b_group, w_router, b_router):
    d = w_group.shape[0]
    gpad = SUBLANES - N_GROUPS
    wr = jnp.concatenate([w_group, jnp.zeros((d, gpad), F32), w_router,
                          jnp.zeros((d, LANES - ROUTER_ROWS), F32)], axis=1)
    br = jnp.concatenate([b_group, jnp.zeros((gpad,), F32), b_router,
                          jnp.zeros((LANES - ROUTER_ROWS,), F32)]).reshape(1, LANES)
    return wr, br


def _moe(x1, g_ffn, logits, w1, w3, w2, layer, tl):
    t, d = x1.shape
    meta, gates, cnt = _route(logits, tr=tl["tr"])

    counts = cnt[:, 0].astype(I32)
    start = jnp.cumsum(counts) - counts
    is_e = meta[0:TOP_K, :, None] == jnp.arange(N_EXPERTS, dtype=I32)
    pos = jnp.sum(jnp.where(is_e, start, 0), axis=-1) + meta[TOP_K:2 * TOP_K]
    xs = _scatter_rows(x1, g_ffn, _tile_pos(pos, tl["tm_scatter"]), tm=tl["tm_scatter"])
    tables = _visit_tables(counts, TOP_K * t, tl["bm"])
    ys = _experts(xs, w1, w3, w2, tables, layer=layer, bm=tl["bm"])
    return x1, gates[0:TOP_K].T, pos, ys


def kernel(x, ln_mix, ln_ffn, ln_final, ev_w_in, ev_conv_w, ev_sgu_ln_g, ev_sgu_ln_b, ev_sgu_w, ev_sgu_b, ev_w_out, od_w_in, od_pool_w, od_pool_scale, od_dw_w, od_dw_b, od_ln_g, od_ln_b, od_w_out, moe_w_group, moe_b_group, moe_w_router, moe_b_router, moe_w1, moe_w3, moe_w2):
    bsz, s, d = x.shape
    assert bsz == 1, "token mixers carry state across row tiles of one sequence"
    depth = ln_mix.shape[0]
    tl = _tiles(s)
    tm = tl["tm_front"]
    x_in = x.reshape(s, d)
    for layer in range(depth):
        i = layer // 2
        if layer > 0:
            x1, gates_t, pos, ys = x_in
            x_in = (x1, gates_t, _tile_pos(pos, tm), ys)
        wr, br = _router_params(moe_w_group[layer], moe_b_group[layer],
                                moe_w_router[layer], moe_b_router[layer])
        if layer % 2 == 0:
            c = ev_conv_w.shape[2]
            mix_params = [ev_conv_w[i], ev_sgu_ln_g[i].reshape(1, c), ev_sgu_ln_b[i].reshape(1, c),
                          ev_sgu_w[i], ev_sgu_b[i].T]
            mix_scratch = [pltpu.VMEM((tm + EVEN_HALO, c), F32)]
            x1, logits = _layer_front(
                "even_layer", _even_mix, x_in, ln_mix[layer], ev_w_in[i].astype(BF16), mix_params,
                mix_scratch, ev_w_out[i].astype(BF16), ln_ffn[layer], wr, br, tm=tm)
        else:
            c = od_dw_w.shape[2]
            mix_params = [od_pool_w[i], od_pool_scale[i].reshape(1, c), od_dw_w[i],
                          od_dw_b[i].reshape(1, c), od_ln_g[i].reshape(1, c), od_ln_b[i].reshape(1, c)]
            gd = c // len(POOL_WINDOWS)
            mix_scratch = [pltpu.VMEM((tm + POOL_HALO, c), F32),
                           pltpu.VMEM((tm + POOL_HALO, gd), F32), pltpu.VMEM((tm + POOL_HALO, gd), F32),
                           pltpu.VMEM((tm + CONF_HALO, c), F32),
                           pltpu.VMEM((SUBLANES - 1, tm + CONF_HALO, LANES), F32),
                           pltpu.VMEM((tm, c), F32)]
            x1, logits = _layer_front(
                "odd_layer", _odd_mix, x_in, ln_mix[layer], od_w_in[i].astype(BF16), mix_params,
                mix_scratch, od_w_out[i].astype(BF16), ln_ffn[layer], wr, br, tm=tm)
        x_in = _moe(x1, ln_ffn[layer], logits, moe_w1, moe_w3, moe_w2, layer, tl)
    x1, gates_t, pos, ys = x_in
    out = _combine(x1, ys, _tile_pos(pos, tl["tm_combine"]), gates_t, ln_final, tm=tl["tm_combine"])
    return out.reshape(bsz, s, d)
```
